```python
import math
import jax, jax.numpy as jnp
from jax import lax
import numpy as np

D_MODEL = 2048
BATCH = 2
SEQ = 4096
DEPTH = 2
DEC_BATCH = 8
DEC_SEQ = 1
PAST_LEN = 16384
PAGE_SIZE = 128

CHUNK = 128
SGU_WIDTH = D_MODEL // 2
SGU_GROUPS = 8
SGU_GROUP_DIM = SGU_WIDTH // SGU_GROUPS
N_HEADS = 8
HEAD_DIM = (D_MODEL // 2) // N_HEADS
ATTN_WIDTH = N_HEADS * HEAD_DIM
ROT_DIM = HEAD_DIM // 4
ROPE_THETA = 500000.0
MOBA_BLOCK = 256
MOBA_TOPK = 3
Q_BLOCK = 32
D_FF = 5632
N_EXPERTS = 8
TOP_K_EXPERTS = 2
D_FF_EXPERT = D_MODEL * 7 // 2
N_DENSE = (DEPTH + 1) // 2
N_MOE = DEPTH // 2
IN_SPLITS = (SGU_WIDTH, 2 * SGU_WIDTH, 2 * SGU_WIDTH + ATTN_WIDTH, 2 * SGU_WIDTH + 2 * ATTN_WIDTH,
             2 * SGU_WIDTH + 3 * ATTN_WIDTH, 2 * SGU_WIDTH + 3 * ATTN_WIDTH + D_MODEL)
IN_WIDTH = 2 * SGU_WIDTH + 3 * ATTN_WIDTH + 2 * D_MODEL
DEEPNORM_ALPHA = (2 * DEPTH) ** 0.25
DEEPNORM_BETA = (8 * DEPTH) ** -0.25
LN_EPS = 1e-5

kernel_name = "hybrid_sgu_moba_deepnorm_step"


def layer_norm(x, g, b):
    xf = x.astype(jnp.float32)
    mu = jnp.mean(xf, axis=-1, keepdims=True)
    var = jnp.mean(jnp.square(xf - mu), axis=-1, keepdims=True)
    y = (xf - mu) * lax.rsqrt(var + LN_EPS) * g.astype(jnp.float32) + b.astype(jnp.float32)
    return y.astype(x.dtype)


def partial_rope(x, pos):
    inv = ROPE_THETA ** (-jnp.arange(0, ROT_DIM, 2, dtype=jnp.float32) / ROT_DIM)
    ang = pos.astype(jnp.float32)[:, None] * inv[None, :]
    cos = jnp.cos(ang)[None, :, None, :]
    sin = jnp.sin(ang)[None, :, None, :]
    xr = x[..., :ROT_DIM].astype(jnp.float32)
    x1, x2 = xr[..., :ROT_DIM // 2], xr[..., ROT_DIM // 2:]
    rot = jnp.concatenate([x1 * cos - x2 * sin, x2 * cos + x1 * sin], axis=-1).astype(x.dtype)
    return jnp.concatenate([rot, x[..., ROT_DIM:]], axis=-1)


def spatial_gating(u, vn, w_s, b_s):
    B, T, _ = u.shape
    n_chunks = -(-T // CHUNK)
    pad = n_chunks * CHUNK - T
    vc = jnp.pad(vn, ((0, 0), (0, pad), (0, 0))).reshape(B, n_chunks, CHUNK, SGU_GROUPS, SGU_GROUP_DIM)
    causal = jnp.tril(jnp.ones((CHUNK, CHUNK), dtype=bool))
    ws = jnp.where(causal[None], w_s, jnp.zeros_like(w_s))
    mixed = jnp.einsum('gts,bcsgd->bctgd', ws, vc) + b_s.T[None, None, :, :, None]
    mixed = mixed.reshape(B, n_chunks * CHUNK, SGU_WIDTH)[:, :T]
    return u * mixed


def moba_attention(q, k, v, q_pos0):
    B, Tq, H, Dh = q.shape
    Tk = k.shape[1]
    nb = max(-(-Tk // MOBA_BLOCK), MOBA_TOPK)
    pad = nb * MOBA_BLOCK - Tk
    kb = jnp.pad(k, ((0, 0), (0, pad), (0, 0), (0, 0))).reshape(B, nb, MOBA_BLOCK, H, Dh).transpose(0, 3, 1, 2, 4)
    vb = jnp.pad(v, ((0, 0), (0, pad), (0, 0), (0, 0))).reshape(B, nb, MOBA_BLOCK, H, Dh).transpose(0, 3, 1, 2, 4)
    kmean = jnp.mean(kb, axis=3, dtype=jnp.float32)
    qb = math.gcd(Tq, Q_BLOCK)
    nq = Tq // qb
    qs = (q * (Dh ** -0.5)).reshape(B, nq, qb, H, Dh).transpose(1, 0, 3, 2, 4)
    b_idx = jnp.arange(B)[:, None, None, None]
    h_idx = jnp.arange(H)[None, :, None, None]
    blk_ids = jnp.arange(nb)
    top_ids = jnp.arange(MOBA_TOPK)

    def one_query_block(args):
        i, qi = args
        start = q_pos0 + i * qb
        qpos = start + jnp.arange(qb)
        own = start // MOBA_BLOCK
        score = jnp.einsum('bhqd,bhnd->bhqn', qi.astype(jnp.float32), kmean)
        score = jnp.where(blk_ids < own, score, -jnp.inf)
        _, sel = lax.top_k(score, MOBA_TOPK)
        valid = top_ids < own
        k_sel = kb[b_idx, h_idx, sel]
        v_sel = vb[b_idx, h_idx, sel]
        k_own = lax.dynamic_index_in_dim(kb, own, axis=2, keepdims=False)
        v_own = lax.dynamic_index_in_dim(vb, own, axis=2, keepdims=False)
        s_sel = jnp.einsum('bhqd,bhqjkd->bhqjk', qi, k_sel, preferred_element_type=jnp.float32)
        s_sel = jnp.where(valid[:, None], s_sel, -jnp.inf)
        s_own = jnp.einsum('bhqd,bhkd->bhqk', qi, k_own, preferred_element_type=jnp.float32)
        kpos = own * MOBA_BLOCK + jnp.arange(MOBA_BLOCK)
        s_own = jnp.where(kpos[None, :] <= qpos[:, None], s_own, -jnp.inf)
        logits = jnp.concatenate([s_sel.reshape(B, H, qb, MOBA_TOPK * MOBA_BLOCK), s_own], axis=-1)
        p = jax.nn.softmax(logits, axis=-1).astype(v.dtype)
        p_sel = p[..., :MOBA_TOPK * MOBA_BLOCK].reshape(B, H, qb, MOBA_TOPK, MOBA_BLOCK)
        p_own = p[..., MOBA_TOPK * MOBA_BLOCK:]
        return (jnp.einsum('bhqjk,bhqjkd->bhqd', p_sel, v_sel)
                + jnp.einsum('bhqk,bhkd->bhqd', p_own, v_own))

    out = lax.map(one_query_block, (jnp.arange(nq), qs))
    return out.transpose(1, 0, 3, 2, 4).reshape(B, Tq, H * Dh)


def token_mixer(h, pos0, past_k, past_v, w_in, ln_v_g, ln_v_b, w_s, b_s, w_ba, w_bb, w_o):
    B, T, _ = h.shape
    proj = h @ w_in
    u, v_a, q, k, v, g_a, g_b = jnp.split(proj, IN_SPLITS, axis=-1)
    u = jax.nn.gelu(u)
    vn = layer_norm(jax.nn.gelu(v_a), ln_v_g, ln_v_b)
    a_out = spatial_gating(u, vn, w_s, b_s) @ w_ba
    pos = pos0 + jnp.arange(T)
    q = partial_rope(q.reshape(B, T, N_HEADS, HEAD_DIM), pos)
    k = partial_rope(k.reshape(B, T, N_HEADS, HEAD_DIM), pos)
    v = v.reshape(B, T, N_HEADS, HEAD_DIM)
    if past_k is None:
        k_all, v_all = k, v
    else:
        k_all = jnp.concatenate([past_k, k], axis=1)
        v_all = jnp.concatenate([past_v, v], axis=1)
    b_out = moba_attention(q, k_all, v_all, pos0) @ w_bb
    merged = jax.nn.sigmoid(g_a) * a_out + jax.nn.sigmoid(g_b) * b_out
    return merged @ w_o, k, v, vn


def swiglu(h, wg, wu, wd):
    return (jax.nn.silu(h @ wg) * (h @ wu)) @ wd


def moe_swiglu(h, router, wg, wu, wd):
    logits = jnp.einsum('btd,de->bte', h, router, preferred_element_type=jnp.float32)
    top_val, top_idx = lax.top_k(logits, TOP_K_EXPERTS)
    top_w = jax.nn.softmax(top_val, axis=-1)
    gates = jnp.einsum('btk,btke->bte', top_w,
                       jax.nn.one_hot(top_idx, N_EXPERTS, dtype=jnp.float32)).astype(h.dtype)
    out = jnp.zeros_like(h)
    for e in range(N_EXPERTS):
        out = out + gates[..., e:e + 1] * swiglu(h, wg[e], wu[e], wd[e])
    return out


def setup_inputs(seed: int = 0) -> dict:
    key = jax.random.key(seed)
    ks = jax.random.split(key, 32)
    f32 = jnp.float32

    def nrm(k, shape, scale):
        return jax.random.normal(k, shape, f32) * scale

    n_pages = PAST_LEN // PAGE_SIZE
    n_used = DEC_BATCH * n_pages
    n_phys = n_used + (n_used + 3) // 4
    page_table = jax.random.permutation(ks[4], n_phys)[:n_used].reshape(DEC_BATCH, n_pages).astype(jnp.int32)
    d_inv = D_MODEL ** -0.5
    return {
        "x_prompt": nrm(ks[0], (BATCH, SEQ, D_MODEL), 1.0),
        "x_sample": nrm(ks[1], (DEC_BATCH, DEC_SEQ, D_MODEL), 1.0),
        "cache_k": nrm(ks[2], (DEPTH, n_phys, PAGE_SIZE, N_HEADS, HEAD_DIM), 1.0),
        "cache_v": nrm(ks[3], (DEPTH, n_phys, PAGE_SIZE, N_HEADS, HEAD_DIM), 1.0),
        "page_table": page_table,
        "c_prompt": nrm(ks[5], (BATCH, D_MODEL), 1.0),
        "c_sample": nrm(ks[6], (DEC_BATCH, D_MODEL), 1.0),
        "w_in": nrm(ks[7], (DEPTH, D_MODEL, IN_WIDTH), d_inv),
        "ln_v_g": 1.0 + nrm(ks[8], (DEPTH, SGU_WIDTH), 0.02),
        "ln_v_b": nrm(ks[9], (DEPTH, SGU_WIDTH), 0.02),
        "w_s": nrm(ks[10], (DEPTH, SGU_GROUPS, CHUNK, CHUNK), CHUNK ** -0.5),
        "b_s": 1.0 + nrm(ks[11], (DEPTH, SGU_GROUPS, CHUNK), 0.02),
        "w_ba": nrm(ks[12], (DEPTH, SGU_WIDTH, D_MODEL), SGU_WIDTH ** -0.5),
        "w_bb": nrm(ks[13], (DEPTH, ATTN_WIDTH, D_MODEL), ATTN_WIDTH ** -0.5),
        "w_o": nrm(ks[14], (DEPTH, D_MODEL, D_MODEL), d_inv * DEEPNORM_BETA),
        "w_ada": nrm(ks[15], (DEPTH, D_MODEL, 6 * D_MODEL), 0.2 * d_inv),
        "b_ada": nrm(ks[16], (DEPTH, 6 * D_MODEL), 0.02),
        "ln1_g": 1.0 + nrm(ks[17], (DEPTH, D_MODEL), 0.02),
        "ln1_b": nrm(ks[18], (DEPTH, D_MODEL), 0.02),
        "ln2_g": 1.0 + nrm(ks[19], (DEPTH, D_MODEL), 0.02),
        "ln2_b": nrm(ks[20], (DEPTH, D_MODEL), 0.02),
        "ffn_w_gate": nrm(ks[21], (N_DENSE, D_MODEL, D_FF), d_inv),
        "ffn_w_up": nrm(ks[22], (N_DENSE, D_MODEL, D_FF), d_inv),
        "ffn_w_down": nrm(ks[23], (N_DENSE, D_FF, D_MODEL), D_FF ** -0.5 * DEEPNORM_BETA),
        "moe_router": nrm(ks[24], (N_MOE, D_MODEL, N_EXPERTS), d_inv),
        "moe_w_gate": nrm(ks[25], (N_MOE, N_EXPERTS, D_MODEL, D_FF_EXPERT), d_inv),
        "moe_w_up": nrm(ks[26], (N_MOE, N_EXPERTS, D_MODEL, D_FF_EXPERT), d_inv),
        "moe_w_down": nrm(ks[27], (N_MOE, N_EXPERTS, D_FF_EXPERT, D_MODEL), D_FF_EXPERT ** -0.5 * DEEPNORM_BETA),
    }


def reference(x_prompt, x_sample, cache_k, cache_v, page_table, c_prompt, c_sample,
              w_in, ln_v_g, ln_v_b, w_s, b_s, w_ba, w_bb, w_o, w_ada, b_ada,
              ln1_g, ln1_b, ln2_g, ln2_b, ffn_w_gate, ffn_w_up, ffn_w_down,
              moe_router, moe_w_gate, moe_w_up, moe_w_down):
    n_seq, n_pages = page_table.shape
    past_len = n_pages * cache_k.shape[2]

    def run(x, c, pos0, use_cache):
        ks, vs, vns = [], [], []
        for l in range(DEPTH):
            mod = (jax.nn.silu(c) @ w_ada[l] + b_ada[l])[:, None, :]
            sh1, sc1, g1, sh2, sc2, g2 = jnp.split(mod, 6, axis=-1)
            h = x * (1 + sc1) + sh1
            if use_cache:
                pk = cache_k[l][page_table].reshape(n_seq, past_len, N_HEADS, HEAD_DIM)
                pv = cache_v[l][page_table].reshape(n_seq, past_len, N_HEADS, HEAD_DIM)
            else:
                pk, pv = None, None
            y, k_new, v_new, vn = token_mixer(h, pos0, pk, pv, w_in[l], ln_v_g[l], ln_v_b[l], w_s[l], b_s[l],
                                              w_ba[l], w_bb[l], w_o[l])
            x = layer_norm(DEEPNORM_ALPHA * x + (1 + g1) * y, ln1_g[l], ln1_b[l])
            h = x * (1 + sc2) + sh2
            if l % 2 == 0:
                f = swiglu(h, ffn_w_gate[l // 2], ffn_w_up[l // 2], ffn_w_down[l // 2])
            else:
                f = moe_swiglu(h, moe_router[l // 2], moe_w_gate[l // 2], moe_w_up[l // 2], moe_w_down[l // 2])
            x = layer_norm(DEEPNORM_ALPHA * x + (1 + g2) * f, ln2_g[l], ln2_b[l])
            ks.append(k_new)
            vs.append(v_new)
            vns.append(vn)
        return x, jnp.stack(ks), jnp.stack(vs), vns

    y_prompt, k_prompt, v_prompt, _ = run(x_prompt, c_prompt, 0, False)
    y_sample, k_sample, v_sample, vn_rows = run(x_sample, c_sample, past_len, True)
    state_sgu_v_sample = jnp.stack(vn_rows)
    return (y_prompt, y_sample, k_prompt, v_prompt, k_sample, v_sample, state_sgu_v_sample)
```

```python
import functools

import jax
import jax.numpy as jnp
from jax import lax
from jax.experimental import pallas as pl
from jax.experimental.pallas import tpu as pltpu

F32 = jnp.float32
BF16 = jnp.bfloat16

CHUNK = 128
SGU_GROUPS = 8
N_HEADS = 8
HEAD_DIM = 128
ROT_DIM = HEAD_DIM // 4
MOBA_BLOCK = 256
MOBA_TOPK = 3
ROPE_THETA = 500000.0
LN_EPS = 1e-5
N_EXPERTS = 8
LANES = 128
ADA_ROWS = 16
VMEM_LIMIT = 56 * 1024 * 1024
NEG_INF = float("-inf")

_NN = (((1,), (0,)), ((), ()))
_NT = (((1,), (1,)), ((), ()))


def _params(*sem):
    return pltpu.CompilerParams(dimension_semantics=sem, vmem_limit_bytes=VMEM_LIMIT)


def _sigmoid(x):
    return 1.0 / (1.0 + jnp.exp(-x))


def _gelu(x):
    return 0.5 * x * (1.0 + jnp.tanh(0.7978845608028654 * (x + 0.044715 * (x * x * x))))


def _layer_norm(x, g, b):
    mu = jnp.mean(x, axis=-1, keepdims=True)
    xc = x - mu
    var = jnp.mean(xc * xc, axis=-1, keepdims=True)
    return xc * lax.rsqrt(var + LN_EPS) * g + b


def _dot(a, b, dims=_NN):
    return lax.dot_general(a, b, dims, preferred_element_type=F32)


def _split_bf16(a):
    hi = a.astype(BF16)
    lo = (a - hi.astype(F32)).astype(BF16)
    return hi, lo


def _dot3(a, b, dims=_NN):
    ah, al = _split_bf16(a)
    bh, bl = _split_bf16(b)
    return _dot(ah, bh, dims) + (_dot(ah, bl, dims) + _dot(al, bh, dims))


def _ada_kernel(c_ref, w_ref, b_ref, o_ref):
    c = c_ref[...]
    a = (c * _sigmoid(c)).astype(BF16)
    o_ref[...] = _dot(a, w_ref[...].astype(BF16)) + b_ref[...]


def _ada(c_all, w_ada, b_ada):
    depth, d, n = w_ada.shape
    bn = 1024
    return pl.pallas_call(
        _ada_kernel,
        out_shape=jax.ShapeDtypeStruct((depth, ADA_ROWS, n), F32),
        grid=(depth, n // bn),
        in_specs=[
            pl.BlockSpec((ADA_ROWS, d), lambda l, j: (0, 0)),
            pl.BlockSpec((None, d, bn), lambda l, j: (l, 0, j)),
            pl.BlockSpec((None, 1, bn), lambda l, j: (l, 0, j)),
        ],
        out_specs=pl.BlockSpec((None, ADA_ROWS, bn), lambda l, j: (l, 0, j)),
        compiler_params=_params("arbitrary", "arbitrary"),
        name="ada",
    )(c_all, w_ada, b_ada.reshape(depth, 1, n))


def _mod_spec(mod3, chunk, d, decode, tiles_per_seq):
    rows = mod3.shape[1]
    if decode:
        return pl.BlockSpec((None, rows, d), lambda i, *_: (0, 0, chunk))
    return pl.BlockSpec((None, rows, d), lambda i, *_: (i // tiles_per_seq, 0, chunk))


def _rope_tables(pos):
    half = ROT_DIM // 2
    inv = ROPE_THETA ** (-jnp.arange(0, ROT_DIM, 2, dtype=F32) / ROT_DIM)
    ang = pos.astype(F32)[:, None] * inv[None, :]
    cos, sin = jnp.cos(ang), jnp.sin(ang)
    r = pos.shape[0]
    ones = jnp.ones((r, HEAD_DIM - ROT_DIM), F32)
    zeros = jnp.zeros((r, HEAD_DIM - ROT_DIM), F32)
    zhalf = jnp.zeros((r, half), F32)
    c = jnp.concatenate([cos, cos, ones], axis=1)
    s_lo = jnp.concatenate([zhalf, sin, zeros], axis=1)
    s_hi = jnp.concatenate([-sin, zhalf, zeros], axis=1)
    return c, s_lo, s_hi


def _proj_kernel(x_ref, sc_ref, sh_ref, w_ref, c_ref, slo_ref, shi_ref, o_ref, h_scr):
    j = pl.program_id(1)

    @pl.when(j == 0)
    def _():
        h_scr[...] = (x_ref[...] * (1.0 + sc_ref[...]) + sh_ref[...]).astype(BF16)

    acc = _dot(h_scr[...], w_ref[...])
    width = acc.shape[1]

    @pl.when(j < 2)
    def _():
        o_ref[...] = _gelu(acc)

    @pl.when((j == 2) | (j == 3))
    def _():
        reps = width // HEAD_DIM
        c = jnp.concatenate([c_ref[...]] * reps, axis=1)
        s_lo = jnp.concatenate([slo_ref[...]] * reps, axis=1)
        s_hi = jnp.concatenate([shi_ref[...]] * reps, axis=1)
        half = ROT_DIM // 2
        o_ref[...] = (acc * c + pltpu.roll(acc, half, 1) * s_lo
                      + pltpu.roll(acc, width - half, 1) * s_hi)

    @pl.when(j == 4)
    def _():
        o_ref[...] = acc

    @pl.when(j > 4)
    def _():
        o_ref[...] = _sigmoid(acc)


def _proj(x, mod3, w_in_b, tabs, *, decode, seq_len):
    m, d = x.shape
    n = w_in_b.shape[1]
    bn = 1024
    assert n == 9 * bn and d == 2 * bn
    bm = m if decode else min(512, seq_len)
    tps = 1 if decode else seq_len // bm
    if decode:
        tab_spec = pl.BlockSpec((bm, HEAD_DIM), lambda i, j: (0, 0))
    else:
        tab_spec = pl.BlockSpec((bm, HEAD_DIM), lambda i, j: (i % tps, 0))
    return pl.pallas_call(
        _proj_kernel,
        out_shape=jax.ShapeDtypeStruct((m, n), F32),
        grid=(m // bm, n // bn),
        in_specs=[
            pl.BlockSpec((bm, d), lambda i, j: (i, 0)),
            _mod_spec(mod3, 1, d, decode, tps),
            _mod_spec(mod3, 0, d, decode, tps),
            pl.BlockSpec((d, bn), lambda i, j: (0, j)),
            tab_spec, tab_spec, tab_spec,
        ],
        out_specs=pl.BlockSpec((bm, bn), lambda i, j: (i, j)),
        scratch_shapes=[pltpu.VMEM((bm, d), BF16)],
        compiler_params=_params("arbitrary", "arbitrary"),
        name="proj_dec" if decode else "proj",
    )(x, mod3, mod3, w_in_b, *tabs)


def _moba_kernel(q_ref, k_ref, v_ref, o_ref, kb_scr, vt_scr, km_scr, bias_scr, *, nb):
    qi = pl.program_id(2)
    blk = MOBA_BLOCK

    @pl.when(qi == 0)
    def _():
        for j in range(nb):
            kj = k_ref[j * blk:(j + 1) * blk, :]
            kb_scr[j * blk:(j + 1) * blk, :] = kj.astype(BF16)
            km_scr[j:j + 1, :] = jnp.sum(kj, axis=0, keepdims=True) * (1.0 / blk)
            vt_scr[j] = v_ref[j * blk:(j + 1) * blk, :].T.astype(BF16)

    qt = (q_ref[...] * (HEAD_DIM ** -0.5)).T
    qtb = qt.astype(BF16)

    score = _dot3(km_scr[...], qt)
    n_idx = lax.broadcasted_iota(jnp.int32, (nb, blk), 0)
    cnt = jnp.zeros((nb, blk), F32)
    for m in range(nb):
        row = score[m:m + 1, :]
        beats = jnp.where(row > score, 1.0, jnp.where(row == score, jnp.where(m < n_idx, 1.0, 0.0), 0.0))
        cnt = cnt + jnp.where(m < qi, beats, 0.0)
    keep = jnp.where(n_idx < qi, jnp.where(cnt < MOBA_TOPK, 1.0, 0.0), 0.0)
    bias_scr[...] = jnp.where(keep > 0.0, 0.0, NEG_INF)

    kd = kb_scr[pl.ds(pl.multiple_of(qi * blk, blk), blk), :]
    s = _dot(kd, qtb)
    key_i = lax.broadcasted_iota(jnp.int32, (blk, blk), 0)
    qry_i = lax.broadcasted_iota(jnp.int32, (blk, blk), 1)
    s = jnp.where(key_i <= qry_i, s, NEG_INF)
    m0 = jnp.max(s, axis=0, keepdims=True)
    p = jnp.exp(s - m0)
    l0 = jnp.sum(p, axis=0, keepdims=True)
    acc0 = _dot(vt_scr[qi], p.astype(BF16))

    def body(j, carry):
        m_run, l_run, acc = carry
        kj = kb_scr[pl.ds(pl.multiple_of(j * blk, blk), blk), :]
        s = _dot(kj, qtb) + bias_scr[pl.ds(j, 1), :]
        m_new = jnp.maximum(m_run, jnp.max(s, axis=0, keepdims=True))
        a = jnp.exp(m_run - m_new)
        p = jnp.exp(s - m_new)
        l_new = a * l_run + jnp.sum(p, axis=0, keepdims=True)
        acc_new = a * acc + _dot(vt_scr[j], p.astype(BF16))
        return m_new, l_new, acc_new

    _, l_fin, acc = lax.fori_loop(0, qi, body, (m0, l0, acc0))
    o_ref[...] = (acc / l_fin).T


def _moba_prefill(proj3):
    b, t, _ = proj3.shape
    blk = MOBA_BLOCK
    nb = t // blk
    assert t % blk == 0 and nb % 8 == 0
    q_col, k_col, v_col = 16, 24, 32
    return pl.pallas_call(
        functools.partial(_moba_kernel, nb=nb),
        out_shape=jax.ShapeDtypeStruct((b, t, N_HEADS * HEAD_DIM), F32),
        grid=(b, N_HEADS, nb),
        in_specs=[
            pl.BlockSpec((None, blk, HEAD_DIM), lambda bi, h, qi: (bi, qi, q_col + h)),
            pl.BlockSpec((None, t, HEAD_DIM), lambda bi, h, qi: (bi, 0, k_col + h)),
            pl.BlockSpec((None, t, HEAD_DIM), lambda bi, h, qi: (bi, 0, v_col + h)),
        ],
        out_specs=pl.BlockSpec((None, blk, HEAD_DIM), lambda bi, h, qi: (bi, qi, h)),
        scratch_shapes=[
            pltpu.VMEM((t, HEAD_DIM), BF16),
            pltpu.VMEM((nb, HEAD_DIM, blk), BF16),
            pltpu.VMEM((nb, HEAD_DIM), F32),
            pltpu.VMEM((nb, blk), F32),
        ],
        compiler_params=_params("arbitrary", "arbitrary", "arbitrary"),
        name="moba",
    )(proj3, proj3, proj3)


def _paged_kmean_kernel(pt_ref, *refs, ppb):
    del pt_ref
    pages, o_ref = refs[:ppb], refs[ppb]
    for b in range(ppb // 2):
        tot = jnp.sum(pages[2 * b][...], axis=0) + jnp.sum(pages[2 * b + 1][...], axis=0)
        o_ref[b] = tot * (1.0 / MOBA_BLOCK)


def _paged_kmean(cache_k, page_table, layer):
    _, _, page, nh, dh = cache_k.shape
    s, n_pages = page_table.shape
    assert 2 * page == MOBA_BLOCK
    ppb = 8 if n_pages % 8 == 0 else 2
    page_specs = [
        pl.BlockSpec((None, None, page, nh, dh),
                     lambda si, c, pt, i=i: (layer, pt[si, c * ppb + i], 0, 0, 0))
        for i in range(ppb)
    ]
    return pl.pallas_call(
        functools.partial(_paged_kmean_kernel, ppb=ppb),
        out_shape=jax.ShapeDtypeStruct((s, n_pages // 2, nh, dh), F32),
        grid_spec=pltpu.PrefetchScalarGridSpec(
            num_scalar_prefetch=1,
            grid=(s, n_pages // ppb),
            in_specs=page_specs,
            out_specs=pl.BlockSpec((None, ppb // 2, nh, dh), lambda si, c, pt: (si, c, 0, 0)),
        ),
        compiler_params=_params("arbitrary", "arbitrary"),
        name="paged_kmean",
    )(page_table, *([cache_k] * ppb))


def _dec_select_kernel(km_ref, q_ref, o_ref, *, n_blocks):
    si = pl.program_id(0)
    q = q_ref[pl.ds(si, 1), :] * (HEAD_DIM ** -0.5)
    lane = lax.broadcasted_iota(jnp.int32, (8, LANES), 1).astype(F32)
    for h in range(N_HEADS):
        qh = jnp.broadcast_to(q[:, h * HEAD_DIM:(h + 1) * HEAD_DIM], (8, HEAD_DIM))
        sc = _dot3(qh, km_ref[h], _NT)
        sc = jnp.where(lane < n_blocks, sc, NEG_INF)
        row = jnp.zeros((8, LANES), F32)
        for r in range(MOBA_TOPK):
            best = jnp.max(sc, axis=1, keepdims=True)
            idx = jnp.min(jnp.where(sc == best, lane, 1e9), axis=1, keepdims=True)
            row = jnp.where(lane == r, idx, row)
            sc = jnp.where(lane == idx, NEG_INF, sc)
        o_ref[pl.ds(h, 1), :] = row[0:1, :].astype(jnp.int32)


def _dec_select(kmean, proj_s):
    s, n_blocks, nh, dh = kmean.shape
    assert MOBA_TOPK <= n_blocks <= LANES
    km = jnp.transpose(kmean, (0, 2, 1, 3))
    km = jnp.pad(km, ((0, 0), (0, 0), (0, LANES - n_blocks), (0, 0)))
    return pl.pallas_call(
        functools.partial(_dec_select_kernel, n_blocks=n_blocks),
        out_shape=jax.ShapeDtypeStruct((s, nh, LANES), jnp.int32),
        grid=(s,),
        in_specs=[
            pl.BlockSpec((None, nh, LANES, dh), lambda si: (si, 0, 0, 0)),
            pl.BlockSpec((s, nh * dh), lambda si: (0, 2)),
        ],
        out_specs=pl.BlockSpec((None, nh, LANES), lambda si: (si, 0, 0)),
        compiler_params=_params("arbitrary"),
        name="dec_select",
    )(km, proj_s)


def _dec_attn_kernel(sel_ref, pt_ref, *refs):
    del sel_ref, pt_ref
    n_sel = 2 * MOBA_TOPK
    k_refs, v_refs = refs[:n_sel], refs[n_sel:2 * n_sel]
    q_ref, kn_ref, vn_ref, o_ref = refs[2 * n_sel:]
    h = pl.program_id(1)
    q = q_ref[...] * (HEAD_DIM ** -0.5)
    is_h = lax.broadcasted_iota(jnp.int32, (N_HEADS, 1), 0) == h
    s_sel = []
    m = jnp.where(is_h, jnp.sum(q * kn_ref[...], axis=-1, keepdims=True), NEG_INF)
    s_own = m
    for i in range(n_sel):
        s_i = jnp.sum(k_refs[i][...] * q[None], axis=-1, keepdims=True)
        s_i = jnp.where(is_h[None], s_i, NEG_INF)
        s_sel.append(s_i)
        m = jnp.maximum(m, jnp.max(s_i, axis=0))
    m = jnp.max(m, axis=0, keepdims=True)
    p_own = jnp.exp(s_own - m)
    denom = p_own
    out = p_own * vn_ref[...]
    for i in range(n_sel):
        p = jnp.exp(s_sel[i] - m)
        denom = denom + jnp.sum(p, axis=0)
        out = out + jnp.sum(p * v_refs[i][...], axis=0)
    out = jnp.sum(out, axis=0, keepdims=True) / jnp.sum(denom, axis=0, keepdims=True)
    o_ref[pl.ds(h, 1), :] = out


def _dec_attn(cache_k, cache_v, page_table, sel, q3, k3, v3, layer):
    _, _, page, nh, dh = cache_k.shape
    s = page_table.shape[0]
    sel_flat = sel[:, :, :MOBA_TOPK].reshape(-1)
    n_sel = 2 * MOBA_TOPK

    def page_spec(i):
        j, r = divmod(i, 2)

        def index_map(si, h, sel_r, pt_r):
            blk = sel_r[(si * nh + h) * MOBA_TOPK + j]
            return (layer, pt_r[si, 2 * blk + r], 0, 0, 0)

        return pl.BlockSpec((None, None, page, nh, dh), index_map)

    row_spec = pl.BlockSpec((None, nh, dh), lambda si, h, sel_r, pt_r: (si, 0, 0))
    return pl.pallas_call(
        _dec_attn_kernel,
        out_shape=jax.ShapeDtypeStruct((s, nh, dh), F32),
        grid_spec=pltpu.PrefetchScalarGridSpec(
            num_scalar_prefetch=2,
            grid=(s, nh),
            in_specs=[page_spec(i) for i in range(n_sel)] * 2 + [row_spec] * 3,
            out_specs=row_spec,
        ),
        compiler_params=_params("arbitrary", "arbitrary"),
        name="dec_attn",
    )(sel_flat, page_table, *([cache_k] * n_sel), *([cache_v] * n_sel), q3, k3, v3)


def _mix_kernel(u_ref, va_ref, attn_ref, ga0_ref, ga1_ref, gb0_ref, gb1_ref, x_ref, g1_ref,
                ws_ref, bs_ref, lvg_ref, lvb_ref, wba_ref, wbb_ref, wo_ref, l1g_ref, l1b_ref,
                *out_refs, decode, alpha):
    vn = _layer_norm(va_ref[...], lvg_ref[...], lvb_ref[...])
    if decode:
        out_refs[1][...] = vn
        mixed = vn * ws_ref[...] + bs_ref[...]
    else:
        rows = vn.shape[0]
        vnb = vn.astype(BF16)
        r_i = lax.broadcasted_iota(jnp.int32, (CHUNK, CHUNK), 0)
        c_i = lax.broadcasted_iota(jnp.int32, (CHUNK, CHUNK), 1)
        w_tril = [jnp.where(c_i <= r_i, ws_ref[g], 0.0).astype(BF16) for g in range(SGU_GROUPS)]
        gd = vn.shape[1] // SGU_GROUPS
        chunks = []
        for c in range(rows // CHUNK):
            cols = []
            for g in range(SGU_GROUPS):
                v_cg = vnb[c * CHUNK:(c + 1) * CHUNK, g * gd:(g + 1) * gd]
                cols.append(_dot(w_tril[g], v_cg) + bs_ref[:, g:g + 1])
            chunks.append(jnp.concatenate(cols, axis=1))
        mixed = jnp.concatenate(chunks, axis=0)
    a_pre = (u_ref[...] * mixed).astype(BF16)
    a_out = _dot(a_pre, wba_ref[...])
    b_out = _dot(attn_ref[...].astype(BF16), wbb_ref[...])
    g_a = jnp.concatenate([ga0_ref[...], ga1_ref[...]], axis=1)
    g_b = jnp.concatenate([gb0_ref[...], gb1_ref[...]], axis=1)
    merged = (g_a * a_out + g_b * b_out).astype(BF16)
    y = _dot(merged, wo_ref[...])
    out_refs[0][...] = _layer_norm(alpha * x_ref[...] + (1.0 + g1_ref[...]) * y, l1g_ref[...], l1b_ref[...])


def _mix(proj, attn, x, mod3, w_s, b_s, ln_v_g, ln_v_b, w_ba_b, w_bb_b, w_o_b, ln1_g, ln1_b,
         *, decode, seq_len, alpha):
    m, d = x.shape
    sw = w_ba_b.shape[0]
    bm = m if decode else min(256, seq_len)
    tps = 1 if decode else seq_len // bm
    const = lambda *_: (0, 0)
    single = pl.Buffered(1)

    def col(cidx):
        return pl.BlockSpec((bm, sw), lambda i: (i, cidx))

    if decode:
        gd = sw // SGU_GROUPS
        ws_arg = jnp.repeat(w_s[:, 0, 0], gd)[None, :]
        bs_arg = jnp.repeat(b_s[:, 0], gd)[None, :]
        ws_spec = pl.BlockSpec((1, sw), const)
        bs_spec = pl.BlockSpec((1, sw), const)
        out_shape = (jax.ShapeDtypeStruct((m, d), F32), jax.ShapeDtypeStruct((m, sw), F32))
        out_specs = (pl.BlockSpec((bm, d), lambda i: (i, 0)), pl.BlockSpec((bm, sw), lambda i: (i, 0)))
    else:
        ws_arg, bs_arg = w_s, b_s.T
        ws_spec = pl.BlockSpec(w_s.shape, lambda i: (0, 0, 0))
        bs_spec = pl.BlockSpec((CHUNK, SGU_GROUPS), const)
        out_shape = (jax.ShapeDtypeStruct((m, d), F32),)
        out_specs = (pl.BlockSpec((bm, d), lambda i: (i, 0)),)

    return pl.pallas_call(
        functools.partial(_mix_kernel, decode=decode, alpha=alpha),
        out_shape=out_shape,
        grid=(m // bm,),
        in_specs=[
            col(0), col(1),
            pl.BlockSpec((bm, sw), lambda i: (i, 0)),
            col(5), col(6), col(7), col(8),
            pl.BlockSpec((bm, d), lambda i: (i, 0)),
            _mod_spec(mod3, 2, d, decode, tps),
            ws_spec, bs_spec,
            pl.BlockSpec((1, sw), const), pl.BlockSpec((1, sw), const),
            pl.BlockSpec((sw, d), const, pipeline_mode=single),
            pl.BlockSpec((sw, d), const, pipeline_mode=single),
            pl.BlockSpec((d, d), const, pipeline_mode=single),
            pl.BlockSpec((1, d), const), pl.BlockSpec((1, d), const),
        ],
        out_specs=out_specs,
        compiler_params=_params("arbitrary"),
        name="mix_dec" if decode else "mix",
    )(proj, proj, attn, proj, proj, proj, proj, x, mod3, ws_arg, bs_arg,
      ln_v_g[None, :], ln_v_b[None, :], w_ba_b, w_bb_b, w_o_b, ln1_g[None, :], ln1_b[None, :])


def _ffn_kernel(x_ref, sc_ref, sh_ref, g_ref, wg_ref, wu_ref, wd_ref, lg_ref, lb_ref, o_ref,
                h_scr, acc_scr, *, alpha):
    f = pl.program_id(1)

    @pl.when(f == 0)
    def _():
        h_scr[...] = (x_ref[...] * (1.0 + sc_ref[...]) + sh_ref[...]).astype(BF16)
        acc_scr[...] = jnp.zeros_like(acc_scr)

    h = h_scr[...]
    a = _dot(h, wg_ref[...])
    act = (a * _sigmoid(a) * _dot(h, wu_ref[...])).astype(BF16)
    acc_scr[...] += _dot(act, wd_ref[...])

    @pl.when(f == pl.num_programs(1) - 1)
    def _():
        y = alpha * x_ref[...] + (1.0 + g_ref[...]) * acc_scr[...]
        o_ref[...] = _layer_norm(y, lg_ref[...], lb_ref[...])


def _ffn(x, mod3, wg_b, wu_b, wd_b, ln_g, ln_b, *, decode, seq_len, alpha):
    m, d = x.shape
    ff = wg_b.shape[1]
    bf = 512 if ff % 512 == 0 else ff
    bm = m if decode else min(512, seq_len)
    tps = 1 if decode else seq_len // bm
    const = lambda *_: (0, 0)
    return pl.pallas_call(
        functools.partial(_ffn_kernel, alpha=alpha),
        out_shape=jax.ShapeDtypeStruct((m, d), F32),
        grid=(m // bm, ff // bf),
        in_specs=[
            pl.BlockSpec((bm, d), lambda i, f: (i, 0)),
            _mod_spec(mod3, 4, d, decode, tps),
            _mod_spec(mod3, 3, d, decode, tps),
            _mod_spec(mod3, 5, d, decode, tps),
            pl.BlockSpec((d, bf), lambda i, f: (0, f)),
            pl.BlockSpec((d, bf), lambda i, f: (0, f)),
            pl.BlockSpec((bf, d), lambda i, f: (f, 0)),
            pl.BlockSpec((1, d), const), pl.BlockSpec((1, d), const),
        ],
        out_specs=pl.BlockSpec((bm, d), lambda i, f: (i, 0)),
        scratch_shapes=[pltpu.VMEM((bm, d), BF16), pltpu.VMEM((bm, d), F32)],
        compiler_params=_params("arbitrary", "arbitrary"),
        name="ffn_dec" if decode else "ffn",
    )(x, mod3, mod3, mod3, wg_b, wu_b, wd_b, ln_g[None, :], ln_b[None, :])


def _moe_kernel(x_ref, sc_ref, sh_ref, g_ref, r_ref, wg_ref, wu_ref, wd_ref, lg_ref, lb_ref, o_ref,
                h_scr, acc_scr, gate_scr, *, alpha):
    e = pl.program_id(1)
    f = pl.program_id(2)
    lane = lax.broadcasted_iota(jnp.int32, gate_scr.shape, 1)

    @pl.when((e == 0) & (f == 0))
    def _():
        h = x_ref[...] * (1.0 + sc_ref[...]) + sh_ref[...]
        h_scr[...] = h.astype(BF16)
        acc_scr[...] = jnp.zeros_like(acc_scr)
        lane_f = lane.astype(F32)
        logits = jnp.where(lane < N_EXPERTS, _dot3(h, r_ref[...]), NEG_INF)
        v1 = jnp.max(logits, axis=1, keepdims=True)
        i1 = jnp.min(jnp.where(logits == v1, lane_f, 1e9), axis=1, keepdims=True)
        rest = jnp.where(lane_f == i1, NEG_INF, logits)
        v2 = jnp.max(rest, axis=1, keepdims=True)
        i2 = jnp.min(jnp.where(rest == v2, lane_f, 1e9), axis=1, keepdims=True)
        ex = jnp.exp(v2 - v1)
        w1 = 1.0 / (1.0 + ex)
        w2 = ex / (1.0 + ex)
        gate_scr[...] = jnp.where(lane_f == i1, w1, 0.0) + jnp.where(lane_f == i2, w2, 0.0)

    h = h_scr[...]
    gate_e = jnp.sum(jnp.where(lane == e, gate_scr[...], 0.0), axis=1, keepdims=True)
    a = _dot(h, wg_ref[...])
    act = (a * _sigmoid(a) * _dot(h, wu_ref[...]) * gate_e).astype(BF16)
    acc_scr[...] += _dot(act, wd_ref[...])

    @pl.when((e == pl.num_programs(1) - 1) & (f == pl.num_programs(2) - 1))
    def _():
        y = alpha * x_ref[...] + (1.0 + g_ref[...]) * acc_scr[...]
        o_ref[...] = _layer_norm(y, lg_ref[...], lb_ref[...])


def _moe(x, mod3, router, wg_b, wu_b, wd_b, ln_g, ln_b, *, decode, seq_len, alpha):
    m, d = x.shape
    ne, _, ff = wg_b.shape
    bf = 512 if ff % 512 == 0 else ff
    bm = m if decode else min(512, seq_len)
    tps = 1 if decode else seq_len // bm
    const = lambda *_: (0, 0)
    router_p = jnp.pad(router, ((0, 0), (0, LANES - ne)))
    return pl.pallas_call(
        functools.partial(_moe_kernel, alpha=alpha),
        out_shape=jax.ShapeDtypeStruct((m, d), F32),
        grid=(m // bm, ne, ff // bf),
        in_specs=[
            pl.BlockSpec((bm, d), lambda i, e, f: (i, 0)),
            _mod_spec(mod3, 4, d, decode, tps),
            _mod_spec(mod3, 3, d, decode, tps),
            _mod_spec(mod3, 5, d, decode, tps),
            pl.BlockSpec((d, LANES), const),
            pl.BlockSpec((None, d, bf), lambda i, e, f: (e, 0, f)),
            pl.BlockSpec((None, d, bf), lambda i, e, f: (e, 0, f)),
            pl.BlockSpec((None, bf, d), lambda i, e, f: (e, f, 0)),
            pl.BlockSpec((1, d), const), pl.BlockSpec((1, d), const),
        ],
        out_specs=pl.BlockSpec((bm, d), lambda i, e, f: (i, 0)),
        scratch_shapes=[pltpu.VMEM((bm, d), BF16), pltpu.VMEM((bm, d), F32), pltpu.VMEM((bm, LANES), F32)],
        compiler_params=_params("arbitrary", "arbitrary", "arbitrary"),
        name="moe_dec" if decode else "moe",
    )(x, mod3, mod3, mod3, router_p, wg_b, wu_b, wd_b, ln_g[None, :], ln_b[None, :])


def kernel(x_prompt, x_sample, cache_k, cache_v, page_table, c_prompt, c_sample,
           w_in, ln_v_g, ln_v_b, w_s, b_s, w_ba, w_bb, w_o, w_ada, b_ada,
           ln1_g, ln1_b, ln2_g, ln2_b, ffn_w_gate, ffn_w_up, ffn_w_down,
           moe_router, moe_w_gate, moe_w_up, moe_w_down):
    depth = w_in.shape[0]
    b, t, d = x_prompt.shape
    s = x_sample.shape[0]
    assert x_sample.shape[1] == 1 and s % 8 == 0 and b + s <= ADA_ROWS
    n_pages, page = page_table.shape[1], cache_k.shape[2]
    past_len = n_pages * page
    alpha = (2 * depth) ** 0.25
    sw = w_ba.shape[1]

    c_all = jnp.concatenate([c_prompt, c_sample, jnp.zeros((ADA_ROWS - b - s, d), F32)], axis=0)
    mod_all = _ada(c_all, w_ada, b_ada)

    tabs_p = _rope_tables(jnp.arange(t))
    tabs_s = _rope_tables(jnp.full((s,), past_len, jnp.int32))

    xp = x_prompt.reshape(b * t, d)
    xs = x_sample.reshape(s, d)
    k_p, v_p, k_s, v_s, vn_s = [], [], [], [], []
    aw = N_HEADS * HEAD_DIM
    k_lo, v_lo = 2 * sw + aw, 2 * sw + 2 * aw
    for l in range(depth):
        mod_p = mod_all[l, :b][:, None, :]
        mod_s = mod_all[l, b:b + s][None]
        w_in_b = w_in[l].astype(BF16)
        mix_w = (w_s[l], b_s[l], ln_v_g[l], ln_v_b[l], w_ba[l].astype(BF16), w_bb[l].astype(BF16),
                 w_o[l].astype(BF16), ln1_g[l], ln1_b[l])
        if l % 2 == 0:
            chan = functools.partial(
                _ffn, wg_b=ffn_w_gate[l // 2].astype(BF16), wu_b=ffn_w_up[l // 2].astype(BF16),
                wd_b=ffn_w_down[l // 2].astype(BF16), ln_g=ln2_g[l], ln_b=ln2_b[l], alpha=alpha)
        else:
            chan = functools.partial(
                _moe, router=moe_router[l // 2], wg_b=moe_w_gate[l // 2].astype(BF16),
                wu_b=moe_w_up[l // 2].astype(BF16), wd_b=moe_w_down[l // 2].astype(BF16),
                ln_g=ln2_g[l], ln_b=ln2_b[l], alpha=alpha)

        proj_p = _proj(xp, mod_p, w_in_b, tabs_p, decode=False, seq_len=t)
        attn_p = _moba_prefill(proj_p.reshape(b, t, -1)).reshape(b * t, -1)
        (xp1,) = _mix(proj_p, attn_p, xp, mod_p, *mix_w, decode=False, seq_len=t, alpha=alpha)
        xp = chan(xp1, mod_p, decode=False, seq_len=t)
        k_p.append(proj_p[:, k_lo:v_lo].reshape(b, t, N_HEADS, HEAD_DIM))
        v_p.append(proj_p[:, v_lo:v_lo + N_HEADS * HEAD_DIM].reshape(b, t, N_HEADS, HEAD_DIM))

        proj_s = _proj(xs, mod_s, w_in_b, tabs_s, decode=True, seq_len=1)
        kmean = _paged_kmean(cache_k, page_table, l)
        sel = _dec_select(kmean, proj_s)
        q3, k3, v3 = (proj_s[:, lo:lo + aw].reshape(s, N_HEADS, HEAD_DIM) for lo in (k_lo - aw, k_lo, v_lo))
        attn_s = _dec_attn(cache_k, cache_v, page_table, sel, q3, k3, v3, l).reshape(s, aw)
        xs1, vn = _mix(proj_s, attn_s, xs, mod_s, *mix_w, decode=True, seq_len=1, alpha=alpha)
        xs = chan(xs1, mod_s, decode=True, seq_len=1)
        k_s.append(proj_s[:, k_lo:v_lo].reshape(s, 1, N_HEADS, HEAD_DIM))
        v_s.append(proj_s[:, v_lo:v_lo + N_HEADS * HEAD_DIM].reshape(s, 1, N_HEADS, HEAD_DIM))
        vn_s.append(vn.reshape(s, 1, sw))

    return (xp.reshape(b, t, d), xs.reshape(s, 1, d), jnp.stack(k_p), jnp.stack(v_p),
            jnp.stack(k_s), jnp.stack(v_s), jnp.stack(vn_s))
```

```python
import functools

import jax
import jax.numpy as jnp
from jax import lax
from jax.experimental import pallas as pl
from jax.experimental.pallas import tpu as pltpu

F32 = jnp.float32
BF16 = jnp.bfloat16

CHUNK = 128
SGU_GROUPS = 8
N_HEADS = 8
HEAD_DIM = 128
ROT_DIM = HEAD_DIM // 4
MOBA_BLOCK = 256
MOBA_TOPK = 3
ROPE_THETA = 500000.0
LN_EPS = 1e-5
N_EXPERTS = 8
LANES = 128
ADA_ROWS = 16
VMEM_LIMIT = 56 * 1024 * 1024
NEG_INF = float("-inf")

_NN = (((1,), (0,)), ((), ()))
_NT = (((1,), (1,)), ((), ()))


def _params(*sem):
    return pltpu.CompilerParams(dimension_semantics=sem, vmem_limit_bytes=VMEM_LIMIT)


def _sigmoid(x):
    return 1.0 / (1.0 + jnp.exp(-x))


def _gelu(x):
    return 0.5 * x * (1.0 + jnp.tanh(0.7978845608028654 * (x + 0.044715 * (x * x * x))))


def _layer_norm(x, g, b):
    mu = jnp.mean(x, axis=-1, keepdims=True)
    xc = x - mu
    var = jnp.mean(xc * xc, axis=-1, keepdims=True)
    return xc * lax.rsqrt(var + LN_EPS) * g + b


def _dot(a, b, dims=_NN):
    return lax.dot_general(a, b, dims, preferred_element_type=F32)


def _split_bf16(a):
    hi = a.astype(BF16)
    lo = (a - hi.astype(F32)).astype(BF16)
    return hi, lo


def _dot3(a, b, dims=_NN):
    ah, al = _split_bf16(a)
    bh, bl = _split_bf16(b)
    return _dot(ah, bh, dims) + (_dot(ah, bl, dims) + _dot(al, bh, dims))


def _ada_kernel(c_ref, w_ref, b_ref, o_ref):
    c = c_ref[...]
    a = (c * _sigmoid(c)).astype(BF16)
    o_ref[...] = _dot(a, w_ref[...].astype(BF16)) + b_ref[...]


def _ada(c_all, w_ada, b_ada):
    depth, d, n = w_ada.shape
    bn = 1024
    return pl.pallas_call(
        _ada_kernel,
        out_shape=jax.ShapeDtypeStruct((depth, ADA_ROWS, n), F32),
        grid=(depth, n // bn),
        in_specs=[
            pl.BlockSpec((ADA_ROWS, d), lambda l, j: (0, 0)),
            pl.BlockSpec((None, d, bn), lambda l, j: (l, 0, j)),
            pl.BlockSpec((None, 1, bn), lambda l, j: (l, 0, j)),
        ],
        out_specs=pl.BlockSpec((None, ADA_ROWS, bn), lambda l, j: (l, 0, j)),
        compiler_params=_params("arbitrary", "arbitrary"),
        name="ada",
    )(c_all, w_ada, b_ada.reshape(depth, 1, n))


def _mod_spec(mod3, chunk, d, decode, tiles_per_seq):
    rows = mod3.shape[1]
    if decode:
        return pl.BlockSpec((None, rows, d), lambda i, *_: (0, 0, chunk))
    return pl.BlockSpec((None, rows, d), lambda i, *_: (i // tiles_per_seq, 0, chunk))


def _rope_tables(pos):
    half = ROT_DIM // 2
    inv = ROPE_THETA ** (-jnp.arange(0, ROT_DIM, 2, dtype=F32) / ROT_DIM)
    ang = pos.astype(F32)[:, None] * inv[None, :]
    cos, sin = jnp.cos(ang), jnp.sin(ang)
    r = pos.shape[0]
    ones = jnp.ones((r, HEAD_DIM - ROT_DIM), F32)
    zeros = jnp.zeros((r, HEAD_DIM - ROT_DIM), F32)
    zhalf = jnp.zeros((r, half), F32)
    c = jnp.concatenate([cos, cos, ones], axis=1)
    s_lo = jnp.concatenate([zhalf, sin, zeros], axis=1)
    s_hi = jnp.concatenate([-sin, zhalf, zeros], axis=1)
    return c, s_lo, s_hi


def _proj_kernel(x_ref, sc_ref, sh_ref, w_ref, c_ref, slo_ref, shi_ref, o_ref, k_ref, v_ref, h_scr):
    j = pl.program_id(1)

    @pl.when(j == 0)
    def _():
        h_scr[...] = (x_ref[...] * (1.0 + sc_ref[...]) + sh_ref[...]).astype(BF16)

    acc = _dot(h_scr[...], w_ref[...])
    width = acc.shape[1]

    def per_head(ref, val):
        for h in range(N_HEADS):
            ref[:, h, :] = val[:, h * HEAD_DIM:(h + 1) * HEAD_DIM]

    @pl.when(j < 2)
    def _():
        o_ref[...] = _gelu(acc)

    @pl.when((j == 2) | (j == 3))
    def _():
        reps = width // HEAD_DIM
        c = jnp.concatenate([c_ref[...]] * reps, axis=1)
        s_lo = jnp.concatenate([slo_ref[...]] * reps, axis=1)
        s_hi = jnp.concatenate([shi_ref[...]] * reps, axis=1)
        half = ROT_DIM // 2
        o_ref[...] = (acc * c + pltpu.roll(acc, half, 1) * s_lo
                      + pltpu.roll(acc, width - half, 1) * s_hi)

    @pl.when(j == 3)
    def _():
        per_head(k_ref, o_ref[...])

    @pl.when(j == 4)
    def _():
        o_ref[...] = acc
        per_head(v_ref, acc)

    @pl.when(j > 4)
    def _():
        o_ref[...] = _sigmoid(acc)


def _proj(x, mod3, w_in_b, tabs, *, decode, seq_len):
    m, d = x.shape
    n = w_in_b.shape[1]
    bn = 1024
    assert n == 9 * bn and d == 2 * bn
    bm = m if decode else min(512, seq_len)
    tps = 1 if decode else seq_len // bm
    if decode:
        tab_spec = pl.BlockSpec((bm, HEAD_DIM), lambda i, j: (0, 0))
    else:
        tab_spec = pl.BlockSpec((bm, HEAD_DIM), lambda i, j: (i % tps, 0))
    kv_shape = jax.ShapeDtypeStruct((m, N_HEADS, HEAD_DIM), F32)
    kv_spec = pl.BlockSpec((bm, N_HEADS, HEAD_DIM), lambda i, j: (i, 0, 0))
    return pl.pallas_call(
        _proj_kernel,
        out_shape=(jax.ShapeDtypeStruct((m, n), F32), kv_shape, kv_shape),
        grid=(m // bm, n // bn),
        in_specs=[
            pl.BlockSpec((bm, d), lambda i, j: (i, 0)),
            _mod_spec(mod3, 1, d, decode, tps),
            _mod_spec(mod3, 0, d, decode, tps),
            pl.BlockSpec((d, bn), lambda i, j: (0, j)),
            tab_spec, tab_spec, tab_spec,
        ],
        out_specs=(pl.BlockSpec((bm, bn), lambda i, j: (i, j)), kv_spec, kv_spec),
        scratch_shapes=[pltpu.VMEM((bm, d), BF16)],
        compiler_params=_params("arbitrary", "arbitrary"),
        name="proj_dec" if decode else "proj",
    )(x, mod3, mod3, w_in_b, *tabs)


MOBA_HEADS_PER_STEP = 2


def _moba_kernel(q_ref, k_ref, v_ref, o_ref, kb_scr, vt_scr, km_scr, bias_scr, *, nb, hp):
    qi = pl.program_id(2)
    blk = MOBA_BLOCK
    pair = 2 * blk
    heads = range(hp)
    cols = [slice(h * HEAD_DIM, (h + 1) * HEAD_DIM) for h in heads]

    @pl.when(qi == 0)
    def _():
        for h in heads:
            for j in range(nb):
                kj = k_ref[j * blk:(j + 1) * blk, cols[h]]
                kb_scr[h, j * blk:(j + 1) * blk, :] = kj.astype(BF16)
                km_scr[h, j:j + 1, :] = jnp.sum(kj, axis=0, keepdims=True) * (1.0 / blk)
            for j in range(nb // 2):
                vt_scr[h, j] = v_ref[j * pair:(j + 1) * pair, cols[h]].T.astype(BF16)

    n_idx = lax.broadcasted_iota(jnp.int32, (nb, blk), 0)
    qtb = []
    for h in heads:
        qt = (q_ref[:, cols[h]] * (HEAD_DIM ** -0.5)).T
        qtb.append(qt.astype(BF16))
        score = _dot3(km_scr[h], qt)
        cnt = jnp.zeros((nb, blk), F32)
        for m in range(nb):
            row = score[m:m + 1, :]
            beats = jnp.where(row > score, 1.0, jnp.where(row == score, jnp.where(m < n_idx, 1.0, 0.0), 0.0))
            cnt = cnt + jnp.where(m < qi, beats, 0.0)
        past = jnp.where(cnt < MOBA_TOPK, 0.0, NEG_INF)
        bias_scr[h] = jnp.where(n_idx < qi, past, jnp.where(n_idx == qi, 0.0, NEG_INF))

    def pair_scores(h, t):
        ks = kb_scr[h, pl.ds(pl.multiple_of(t * pair, pair), pair), :]
        bias = jnp.concatenate([jnp.broadcast_to(bias_scr[h, pl.ds(2 * t, 1), :], (blk, blk)),
                                jnp.broadcast_to(bias_scr[h, pl.ds(2 * t + 1, 1), :], (blk, blk))], axis=0)
        return _dot(ks, qtb[h]) + bias

    t_own = lax.shift_right_logical(qi, 1)
    rel = lax.broadcasted_iota(jnp.int32, (pair, blk), 0) - (qi - 2 * t_own) * blk
    qry_i = lax.broadcasted_iota(jnp.int32, (pair, blk), 1)
    init = []
    for h in heads:
        s = pair_scores(h, t_own)
        s = jnp.where(rel > qry_i, jnp.where(rel < blk, NEG_INF, s), s)
        m0 = jnp.max(s, axis=0, keepdims=True)
        p = jnp.exp(s - m0)
        init += [m0, jnp.sum(p, axis=0, keepdims=True), _dot(vt_scr[h, t_own], p.astype(BF16))]

    def body(t, carry):
        out = []
        for h in heads:
            m_run, l_run, acc = carry[3 * h:3 * h + 3]
            s = pair_scores(h, t)
            m_new = jnp.maximum(m_run, jnp.max(s, axis=0, keepdims=True))
            a = jnp.exp(m_run - m_new)
            p = jnp.exp(s - m_new)
            out += [m_new, a * l_run + jnp.sum(p, axis=0, keepdims=True),
                    a * acc + _dot(vt_scr[h, t], p.astype(BF16))]
        return tuple(out)

    fin = lax.fori_loop(0, t_own, body, tuple(init))
    for h in heads:
        o_ref[:, cols[h]] = (fin[3 * h + 2] / fin[3 * h + 1]).T


def _moba_prefill(proj3):
    b, t, _ = proj3.shape
    blk = MOBA_BLOCK
    nb = t // blk
    hp = MOBA_HEADS_PER_STEP
    assert t % blk == 0 and nb % 8 == 0 and N_HEADS % hp == 0
    w = hp * HEAD_DIM
    q_col, k_col, v_col = (c * HEAD_DIM // w for c in (16, 24, 32))
    return pl.pallas_call(
        functools.partial(_moba_kernel, nb=nb, hp=hp),
        out_shape=jax.ShapeDtypeStruct((b, t, N_HEADS * HEAD_DIM), F32),
        grid=(b, N_HEADS // hp, nb),
        in_specs=[
            pl.BlockSpec((None, blk, w), lambda bi, h, qi: (bi, qi, q_col + h)),
            pl.BlockSpec((None, t, w), lambda bi, h, qi: (bi, 0, k_col + h)),
            pl.BlockSpec((None, t, w), lambda bi, h, qi: (bi, 0, v_col + h)),
        ],
        out_specs=pl.BlockSpec((None, blk, w), lambda bi, h, qi: (bi, qi, h)),
        scratch_shapes=[
            pltpu.VMEM((hp, t, HEAD_DIM), BF16),
            pltpu.VMEM((hp, nb // 2, HEAD_DIM, 2 * blk), BF16),
            pltpu.VMEM((hp, nb, HEAD_DIM), F32),
            pltpu.VMEM((hp, nb, blk), F32),
        ],
        compiler_params=_params("arbitrary", "arbitrary", "arbitrary"),
        name="moba",
    )(proj3, proj3, proj3)


def _paged_kmean_kernel(pt_ref, *refs, ppb):
    del pt_ref
    pages, o_ref = refs[:ppb], refs[ppb]
    for b in range(ppb // 2):
        tot = jnp.sum(pages[2 * b][...], axis=0) + jnp.sum(pages[2 * b + 1][...], axis=0)
        o_ref[b] = tot * (1.0 / MOBA_BLOCK)


def _paged_kmean(cache_k, page_table, layer):
    _, _, page, nh, dh = cache_k.shape
    s, n_pages = page_table.shape
    assert 2 * page == MOBA_BLOCK
    ppb = 16 if n_pages % 16 == 0 else 2
    page_specs = [
        pl.BlockSpec((None, None, page, nh, dh),
                     lambda si, c, pt, i=i: (layer, pt[si, c * ppb + i], 0, 0, 0))
        for i in range(ppb)
    ]
    return pl.pallas_call(
        functools.partial(_paged_kmean_kernel, ppb=ppb),
        out_shape=jax.ShapeDtypeStruct((s, n_pages // 2, nh, dh), F32),
        grid_spec=pltpu.PrefetchScalarGridSpec(
            num_scalar_prefetch=1,
            grid=(s, n_pages // ppb),
            in_specs=page_specs,
            out_specs=pl.BlockSpec((None, ppb // 2, nh, dh), lambda si, c, pt: (si, c, 0, 0)),
        ),
        compiler_params=_params("arbitrary", "arbitrary"),
        name="paged_kmean",
    )(page_table, *([cache_k] * ppb))


def _dec_select_kernel(km_ref, q_ref, o_ref, *, n_blocks):
    si = pl.program_id(0)
    q = q_ref[pl.ds(si, 1), :] * (HEAD_DIM ** -0.5)
    lane = lax.broadcasted_iota(jnp.int32, (8, LANES), 1).astype(F32)
    for h in range(N_HEADS):
        qh = jnp.broadcast_to(q[:, h * HEAD_DIM:(h + 1) * HEAD_DIM], (8, HEAD_DIM))
        sc = _dot3(qh, km_ref[h], _NT)
        sc = jnp.where(lane < n_blocks, sc, NEG_INF)
        row = jnp.zeros((8, LANES), F32)
        for r in range(MOBA_TOPK):
            best = jnp.max(sc, axis=1, keepdims=True)
            idx = jnp.min(jnp.where(sc == best, lane, 1e9), axis=1, keepdims=True)
            row = jnp.where(lane == r, idx, row)
            sc = jnp.where(lane == idx, NEG_INF, sc)
        o_ref[pl.ds(h, 1), :] = row[0:1, :].astype(jnp.int32)


def _dec_select(kmean, proj_s):
    s, n_blocks, nh, dh = kmean.shape
    assert MOBA_TOPK <= n_blocks <= LANES
    km = jnp.transpose(kmean, (0, 2, 1, 3))
    km = jnp.pad(km, ((0, 0), (0, 0), (0, LANES - n_blocks), (0, 0)))
    return pl.pallas_call(
        functools.partial(_dec_select_kernel, n_blocks=n_blocks),
        out_shape=jax.ShapeDtypeStruct((s, nh, LANES), jnp.int32),
        grid=(s,),
        in_specs=[
            pl.BlockSpec((None, nh, LANES, dh), lambda si: (si, 0, 0, 0)),
            pl.BlockSpec((s, nh * dh), lambda si: (0, 2)),
        ],
        out_specs=pl.BlockSpec((None, nh, LANES), lambda si: (si, 0, 0)),
        compiler_params=_params("arbitrary"),
        name="dec_select",
    )(km, proj_s)


def _dec_attn_kernel(sel_ref, pt_ref, *refs):
    del sel_ref, pt_ref
    n_sel = 2 * MOBA_TOPK
    k_refs, v_refs = refs[:n_sel], refs[n_sel:2 * n_sel]
    q_ref, kn_ref, vn_ref, o_ref = refs[2 * n_sel:]
    h = pl.program_id(1)
    q = q_ref[...] * (HEAD_DIM ** -0.5)
    is_h = lax.broadcasted_iota(jnp.int32, (N_HEADS, 1), 0) == h
    s_sel = []
    m = jnp.where(is_h, jnp.sum(q * kn_ref[...], axis=-1, keepdims=True), NEG_INF)
    s_own = m
    for i in range(n_sel):
        s_i = jnp.sum(k_refs[i][...] * q[None], axis=-1, keepdims=True)
        s_i = jnp.where(is_h[None], s_i, NEG_INF)
        s_sel.append(s_i)
        m = jnp.maximum(m, jnp.max(s_i, axis=0))
    m = jnp.max(m, axis=0, keepdims=True)
    p_own = jnp.exp(s_own - m)
    denom = p_own
    out = p_own * vn_ref[...]
    for i in range(n_sel):
        p = jnp.exp(s_sel[i] - m)
        denom = denom + jnp.sum(p, axis=0)
        out = out + jnp.sum(p * v_refs[i][...], axis=0)
    out = jnp.sum(out, axis=0, keepdims=True) / jnp.sum(denom, axis=0, keepdims=True)
    o_ref[pl.ds(h, 1), :] = out


def _dec_attn(cache_k, cache_v, page_table, sel, q3, k3, v3, layer):
    _, _, page, nh, dh = cache_k.shape
    s = page_table.shape[0]
    sel_flat = sel[:, :, :MOBA_TOPK].reshape(-1)
    n_sel = 2 * MOBA_TOPK

    def page_spec(i):
        j, r = divmod(i, 2)

        def index_map(si, h, sel_r, pt_r):
            blk = sel_r[(si * nh + h) * MOBA_TOPK + j]
            return (layer, pt_r[si, 2 * blk + r], 0, 0, 0)

        return pl.BlockSpec((None, None, page, nh, dh), index_map)

    row_spec = pl.BlockSpec((None, nh, dh), lambda si, h, sel_r, pt_r: (si, 0, 0))
    return pl.pallas_call(
        _dec_attn_kernel,
        out_shape=jax.ShapeDtypeStruct((s, nh, dh), F32),
        grid_spec=pltpu.PrefetchScalarGridSpec(
            num_scalar_prefetch=2,
            grid=(s, nh),
            in_specs=[page_spec(i) for i in range(n_sel)] * 2 + [row_spec] * 3,
            out_specs=row_spec,
        ),
        compiler_params=_params("arbitrary", "arbitrary"),
        name="dec_attn",
    )(sel_flat, page_table, *([cache_k] * n_sel), *([cache_v] * n_sel), q3, k3, v3)


def _mix_kernel(u_ref, va_ref, attn_ref, ga0_ref, ga1_ref, gb0_ref, gb1_ref, x_ref, g1_ref,
                ws_ref, bs_ref, lvg_ref, lvb_ref, wba_ref, wbb_ref, wo_ref, l1g_ref, l1b_ref,
                *out_refs, decode, alpha):
    vn = _layer_norm(va_ref[...], lvg_ref[...], lvb_ref[...])
    if decode:
        out_refs[1][...] = vn
        mixed = vn * ws_ref[...] + bs_ref[...]
    else:
        rows = vn.shape[0]
        vnb = vn.astype(BF16)
        r_i = lax.broadcasted_iota(jnp.int32, (CHUNK, CHUNK), 0)
        c_i = lax.broadcasted_iota(jnp.int32, (CHUNK, CHUNK), 1)
        w_tril = [jnp.where(c_i <= r_i, ws_ref[g], 0.0).astype(BF16) for g in range(SGU_GROUPS)]
        gd = vn.shape[1] // SGU_GROUPS
        chunks = []
        for c in range(rows // CHUNK):
            cols = []
            for g in range(SGU_GROUPS):
                v_cg = vnb[c * CHUNK:(c + 1) * CHUNK, g * gd:(g + 1) * gd]
                cols.append(_dot(w_tril[g], v_cg) + bs_ref[:, g:g + 1])
            chunks.append(jnp.concatenate(cols, axis=1))
        mixed = jnp.concatenate(chunks, axis=0)
    a_pre = (u_ref[...] * mixed).astype(BF16)
    a_out = _dot(a_pre, wba_ref[...])
    b_out = _dot(attn_ref[...].astype(BF16), wbb_ref[...])
    g_a = jnp.concatenate([ga0_ref[...], ga1_ref[...]], axis=1)
    g_b = jnp.concatenate([gb0_ref[...], gb1_ref[...]], axis=1)
    merged = (g_a * a_out + g_b * b_out).astype(BF16)
    y = _dot(merged, wo_ref[...])
    out_refs[0][...] = _layer_norm(alpha * x_ref[...] + (1.0 + g1_ref[...]) * y, l1g_ref[...], l1b_ref[...])


def _mix(proj, attn, x, mod3, w_s, b_s, ln_v_g, ln_v_b, w_ba_b, w_bb_b, w_o_b, ln1_g, ln1_b,
         *, decode, seq_len, alpha):
    m, d = x.shape
    sw = w_ba_b.shape[0]
    bm = m if decode else min(256, seq_len)
    tps = 1 if decode else seq_len // bm
    const = lambda *_: (0, 0)
    single = pl.Buffered(1)

    def col(cidx):
        return pl.BlockSpec((bm, sw), lambda i: (i, cidx))

    if decode:
        gd = sw // SGU_GROUPS
        ws_arg = jnp.repeat(w_s[:, 0, 0], gd)[None, :]
        bs_arg = jnp.repeat(b_s[:, 0], gd)[None, :]
        ws_spec = pl.BlockSpec((1, sw), const)
        bs_spec = pl.BlockSpec((1, sw), const)
        out_shape = (jax.ShapeDtypeStruct((m, d), F32), jax.ShapeDtypeStruct((m, sw), F32))
        out_specs = (pl.BlockSpec((bm, d), lambda i: (i, 0)), pl.BlockSpec((bm, sw), lambda i: (i, 0)))
    else:
        ws_arg, bs_arg = w_s, b_s.T
        ws_spec = pl.BlockSpec(w_s.shape, lambda i: (0, 0, 0))
        bs_spec = pl.BlockSpec((CHUNK, SGU_GROUPS), const)
        out_shape = (jax.ShapeDtypeStruct((m, d), F32),)
        out_specs = (pl.BlockSpec((bm, d), lambda i: (i, 0)),)

    return pl.pallas_call(
        functools.partial(_mix_kernel, decode=decode, alpha=alpha),
        out_shape=out_shape,
        grid=(m // bm,),
        in_specs=[
            col(0), col(1),
            pl.BlockSpec((bm, sw), lambda i: (i, 0)),
            col(5), col(6), col(7), col(8),
            pl.BlockSpec((bm, d), lambda i: (i, 0)),
            _mod_spec(mod3, 2, d, decode, tps),
            ws_spec, bs_spec,
            pl.BlockSpec((1, sw), const), pl.BlockSpec((1, sw), const),
            pl.BlockSpec((sw, d), const, pipeline_mode=single),
            pl.BlockSpec((sw, d), const, pipeline_mode=single),
            pl.BlockSpec((d, d), const, pipeline_mode=single),
            pl.BlockSpec((1, d), const), pl.BlockSpec((1, d), const),
        ],
        out_specs=out_specs,
        compiler_params=_params("arbitrary"),
        name="mix_dec" if decode else "mix",
    )(proj, proj, attn, proj, proj, proj, proj, x, mod3, ws_arg, bs_arg,
      ln_v_g[None, :], ln_v_b[None, :], w_ba_b, w_bb_b, w_o_b, ln1_g[None, :], ln1_b[None, :])


def _ffn_kernel(x_ref, sc_ref, sh_ref, g_ref, wg_ref, wu_ref, wd_ref, lg_ref, lb_ref, o_ref,
                h_scr, acc_scr, *, alpha):
    f = pl.program_id(1)

    @pl.when(f == 0)
    def _():
        h_scr[...] = (x_ref[...] * (1.0 + sc_ref[...]) + sh_ref[...]).astype(BF16)
        acc_scr[...] = jnp.zeros_like(acc_scr)

    h = h_scr[...]
    a = _dot(h, wg_ref[...])
    act = (a * _sigmoid(a) * _dot(h, wu_ref[...])).astype(BF16)
    acc_scr[...] += _dot(act, wd_ref[...])

    @pl.when(f == pl.num_programs(1) - 1)
    def _():
        y = alpha * x_ref[...] + (1.0 + g_ref[...]) * acc_scr[...]
        o_ref[...] = _layer_norm(y, lg_ref[...], lb_ref[...])


def _ffn(x, mod3, wg_b, wu_b, wd_b, ln_g, ln_b, *, decode, seq_len, alpha):
    m, d = x.shape
    ff = wg_b.shape[1]
    bf = 512 if ff % 512 == 0 else ff
    bm = m if decode else min(512, seq_len)
    tps = 1 if decode else seq_len // bm
    const = lambda *_: (0, 0)
    return pl.pallas_call(
        functools.partial(_ffn_kernel, alpha=alpha),
        out_shape=jax.ShapeDtypeStruct((m, d), F32),
        grid=(m // bm, ff // bf),
        in_specs=[
            pl.BlockSpec((bm, d), lambda i, f: (i, 0)),
            _mod_spec(mod3, 4, d, decode, tps),
            _mod_spec(mod3, 3, d, decode, tps),
            _mod_spec(mod3, 5, d, decode, tps),
            pl.BlockSpec((d, bf), lambda i, f: (0, f)),
            pl.BlockSpec((d, bf), lambda i, f: (0, f)),
            pl.BlockSpec((bf, d), lambda i, f: (f, 0)),
            pl.BlockSpec((1, d), const), pl.BlockSpec((1, d), const),
        ],
        out_specs=pl.BlockSpec((bm, d), lambda i, f: (i, 0)),
        scratch_shapes=[pltpu.VMEM((bm, d), BF16), pltpu.VMEM((bm, d), F32)],
        compiler_params=_params("arbitrary", "arbitrary"),
        name="ffn_dec" if decode else "ffn",
    )(x, mod3, mod3, mod3, wg_b, wu_b, wd_b, ln_g[None, :], ln_b[None, :])


def _moe_route_kernel(x_ref, sc_ref, sh_ref, r_ref, h_ref, route_ref, cnt_ref, cnt_scr):
    i = pl.program_id(0)
    rows = x_ref.shape[0]

    @pl.when(i == 0)
    def _():
        cnt_scr[...] = jnp.zeros_like(cnt_scr)

    h = x_ref[...] * (1.0 + sc_ref[...]) + sh_ref[...]
    h_ref[...] = h
    lane = lax.broadcasted_iota(jnp.int32, (rows, LANES), 1)
    lane_f = lane.astype(F32)
    logits = jnp.where(lane < N_EXPERTS, _dot3(h, r_ref[...]), NEG_INF)
    v1 = jnp.max(logits, axis=1, keepdims=True)
    i1 = jnp.min(jnp.where(logits == v1, lane_f, 1e9), axis=1, keepdims=True)
    rest = jnp.where(lane_f == i1, NEG_INF, logits)
    v2 = jnp.max(rest, axis=1, keepdims=True)
    i2 = jnp.min(jnp.where(rest == v2, lane_f, 1e9), axis=1, keepdims=True)
    ex = jnp.exp(v2 - v1)
    w1 = 1.0 / (1.0 + ex)
    w2 = ex / (1.0 + ex)
    picks = jnp.where(lane_f == i1, 1.0, jnp.where(lane_f == i2, 1.0, 0.0))
    if rows >= LANES:
        r_i = lax.broadcasted_iota(jnp.int32, (rows, rows), 0)
        c_i = lax.broadcasted_iota(jnp.int32, (rows, rows), 1)
        before = _dot(jnp.where(c_i < r_i, 1.0, 0.0).astype(BF16), picks.astype(BF16))
    else:
        r_i = lax.broadcasted_iota(jnp.int32, (rows, LANES), 0)
        before = jnp.zeros((rows, LANES), F32)
        for t in range(rows - 1):
            before = before + jnp.where(r_i > t, picks[t:t + 1, :], 0.0)
    before = before + cnt_scr[0:1, :]
    rank1 = jnp.sum(jnp.where(lane_f == i1, before, 0.0), axis=1, keepdims=True)
    rank2 = jnp.sum(jnp.where(lane_f == i2, before, 0.0), axis=1, keepdims=True)
    rec = jnp.zeros((rows, LANES), F32)
    for k, val in enumerate((i1, i2, w1, w2, rank1, rank2)):
        rec = jnp.where(lane == k, val, rec)
    route_ref[...] = rec
    cnt_scr[...] = cnt_scr[...] + jnp.sum(picks, axis=0, keepdims=True)
    cnt_ref[...] = cnt_scr[...]


def _moe_route(x, mod3, router, *, decode, seq_len):
    m, d = x.shape
    bm = m if decode else min(512, seq_len)
    tps = 1 if decode else seq_len // bm
    router_p = jnp.pad(router, ((0, 0), (0, LANES - router.shape[1])))
    return pl.pallas_call(
        _moe_route_kernel,
        out_shape=(jax.ShapeDtypeStruct((m, d), F32), jax.ShapeDtypeStruct((m, LANES), F32),
                   jax.ShapeDtypeStruct((8, LANES), F32)),
        grid=(m // bm,),
        in_specs=[
            pl.BlockSpec((bm, d), lambda i: (i, 0)),
            _mod_spec(mod3, 4, d, decode, tps),
            _mod_spec(mod3, 3, d, decode, tps),
            pl.BlockSpec((d, LANES), lambda i: (0, 0)),
        ],
        out_specs=(pl.BlockSpec((bm, d), lambda i: (i, 0)), pl.BlockSpec((bm, LANES), lambda i: (i, 0)),
                   pl.BlockSpec((8, LANES), lambda i: (0, 0))),
        scratch_shapes=[pltpu.VMEM((8, LANES), F32)],
        compiler_params=_params("arbitrary"),
        name="moe_route_dec" if decode else "moe_route",
    )(x, mod3, mod3, router_p)


def _row_gather_start(src_hbm, idx_ref, base, dst, sem):
    def issue(r, carry):
        pltpu.make_async_copy(src_hbm.at[pl.ds(idx_ref[base + r], 1), :], dst.at[pl.ds(r, 1), :], sem).start()
        return carry

    lax.fori_loop(0, dst.shape[0], issue, 0)


def _row_gather_wait(src_hbm, dst, sem):
    pltpu.make_async_copy(src_hbm.at[pl.ds(0, dst.shape[0]), :], dst, sem).wait()


def _moe_group_kernel(te_ref, tv_ref, tok_ref, h_hbm, wg_ref, wu_ref, wd_ref, y_ref,
                      xg_scr, hb_scr, acc_scr, sem):
    del te_ref
    i = pl.program_id(0)
    f = pl.program_id(1)
    rows = xg_scr.shape[0]
    valid = tv_ref[i] == 1
    last = f == pl.num_programs(1) - 1

    @pl.when(valid & (f == 0))
    def _():
        _row_gather_start(h_hbm, tok_ref, i * rows, xg_scr, sem)
        _row_gather_wait(h_hbm, xg_scr, sem)
        hb_scr[...] = xg_scr[...].astype(BF16)
        acc_scr[...] = jnp.zeros_like(acc_scr)

    @pl.when(valid)
    def _():
        h = hb_scr[...]
        a = _dot(h, wg_ref[...])
        act = (a * _sigmoid(a) * _dot(h, wu_ref[...])).astype(BF16)
        acc_scr[...] += _dot(act, wd_ref[...])

    @pl.when(valid & last)
    def _():
        y_ref[...] = acc_scr[...]

    @pl.when(jnp.logical_not(valid) & last)
    def _():
        y_ref[...] = jnp.zeros_like(y_ref)


def _moe_group(h2, tile_expert, tile_valid, tok_of_slot, wg_b, wu_b, wd_b, *, rows, decode):
    d = h2.shape[1]
    ff = wg_b.shape[2]
    bf = 512 if ff % 512 == 0 else ff
    nf = ff // bf
    n_tiles = tile_expert.shape[0]

    def f_idx(i, f, tv):
        return jnp.where(tv[i] == 1, f, nf - 1)

    return pl.pallas_call(
        _moe_group_kernel,
        out_shape=jax.ShapeDtypeStruct((n_tiles * rows, d), F32),
        grid_spec=pltpu.PrefetchScalarGridSpec(
            num_scalar_prefetch=3,
            grid=(n_tiles, nf),
            in_specs=[
                pl.BlockSpec(memory_space=pl.ANY),
                pl.BlockSpec((None, d, bf), lambda i, f, te, tv, tok: (te[i], 0, f_idx(i, f, tv))),
                pl.BlockSpec((None, d, bf), lambda i, f, te, tv, tok: (te[i], 0, f_idx(i, f, tv))),
                pl.BlockSpec((None, bf, d), lambda i, f, te, tv, tok: (te[i], f_idx(i, f, tv), 0)),
            ],
            out_specs=pl.BlockSpec((rows, d), lambda i, f, te, tv, tok: (i, 0)),
            scratch_shapes=[pltpu.VMEM((rows, d), F32), pltpu.VMEM((rows, d), BF16),
                            pltpu.VMEM((rows, d), F32), pltpu.SemaphoreType.DMA],
        ),
        compiler_params=_params("arbitrary", "arbitrary"),
        name="moe_group_dec" if decode else "moe_group",
    )(tile_expert, tile_valid, tok_of_slot, h2, wg_b, wu_b, wd_b)


def _moe_combine_kernel(s1_ref, s2_ref, y_hbm, route_ref, x_ref, g_ref, lg_ref, lb_ref, o_ref,
                        y1_scr, y2_scr, sem1, sem2, *, alpha):
    i = pl.program_id(0)
    rows = x_ref.shape[0]
    _row_gather_start(y_hbm, s1_ref, i * rows, y1_scr, sem1)
    _row_gather_start(y_hbm, s2_ref, i * rows, y2_scr, sem2)
    _row_gather_wait(y_hbm, y1_scr, sem1)
    _row_gather_wait(y_hbm, y2_scr, sem2)
    route = route_ref[...]
    f = route[:, 2:3] * y1_scr[...] + route[:, 3:4] * y2_scr[...]
    y = alpha * x_ref[...] + (1.0 + g_ref[...]) * f
    o_ref[...] = _layer_norm(y, lg_ref[...], lb_ref[...])


def _moe_combine(y_sorted, slot1, slot2, route, x, mod3, ln_g, ln_b, *, decode, seq_len, alpha):
    m, d = x.shape
    bm = m if decode else min(256, seq_len)
    tps = 1 if decode else seq_len // bm
    rows_mod = mod3.shape[1]
    if decode:
        g_spec = pl.BlockSpec((None, rows_mod, d), lambda i, s1, s2: (0, 0, 5))
    else:
        g_spec = pl.BlockSpec((None, rows_mod, d), lambda i, s1, s2: (i // tps, 0, 5))
    const = lambda i, s1, s2: (0, 0)
    return pl.pallas_call(
        functools.partial(_moe_combine_kernel, alpha=alpha),
        out_shape=jax.ShapeDtypeStruct((m, d), F32),
        grid_spec=pltpu.PrefetchScalarGridSpec(
            num_scalar_prefetch=2,
            grid=(m // bm,),
            in_specs=[
                pl.BlockSpec(memory_space=pl.ANY),
                pl.BlockSpec((bm, LANES), lambda i, s1, s2: (i, 0)),
                pl.BlockSpec((bm, d), lambda i, s1, s2: (i, 0)),
                g_spec,
                pl.BlockSpec((1, d), const), pl.BlockSpec((1, d), const),
            ],
            out_specs=pl.BlockSpec((bm, d), lambda i, s1, s2: (i, 0)),
            scratch_shapes=[pltpu.VMEM((bm, d), F32), pltpu.VMEM((bm, d), F32),
                            pltpu.SemaphoreType.DMA, pltpu.SemaphoreType.DMA],
        ),
        compiler_params=_params("arbitrary"),
        name="moe_combine_dec" if decode else "moe_combine",
    )(slot1, slot2, y_sorted, route, x, mod3, ln_g[None, :], ln_b[None, :])


def _moe(x, mod3, router, wg_b, wu_b, wd_b, ln_g, ln_b, *, decode, seq_len, alpha):
    m = x.shape[0]
    ne = wg_b.shape[0]
    rows = 16 if decode else 512
    n_tiles = -(-2 * m // rows) + ne
    h2, route, counts = _moe_route(x, mod3, router, decode=decode, seq_len=seq_len)

    cnt = counts[0, :ne].astype(jnp.int32)
    padded = (cnt + rows - 1) // rows * rows
    ends = jnp.cumsum(padded)
    starts = ends - padded
    e1, e2 = route[:, 0].astype(jnp.int32), route[:, 1].astype(jnp.int32)
    slot1 = starts[e1] + route[:, 4].astype(jnp.int32)
    slot2 = starts[e2] + route[:, 5].astype(jnp.int32)
    tok = jnp.arange(m, dtype=jnp.int32)
    tok_of_slot = jnp.zeros((n_tiles * rows,), jnp.int32).at[slot1].set(tok).at[slot2].set(tok)
    tile_lo = jnp.arange(n_tiles, dtype=jnp.int32) * rows
    tile_valid = (tile_lo < ends[-1]).astype(jnp.int32)
    last_lo = jnp.maximum(ends[-1] - rows, 0)
    tile_expert = jnp.searchsorted(ends, jnp.minimum(tile_lo, last_lo), side="right").astype(jnp.int32)
    tile_expert = jnp.minimum(tile_expert, ne - 1)

    y_sorted = _moe_group(h2, tile_expert, tile_valid, tok_of_slot, wg_b, wu_b, wd_b, rows=rows, decode=decode)
    return _moe_combine(y_sorted, slot1, slot2, route, x, mod3, ln_g, ln_b,
                        decode=decode, seq_len=seq_len, alpha=alpha)


def kernel(x_prompt, x_sample, cache_k, cache_v, page_table, c_prompt, c_sample,
           w_in, ln_v_g, ln_v_b, w_s, b_s, w_ba, w_bb, w_o, w_ada, b_ada,
           ln1_g, ln1_b, ln2_g, ln2_b, ffn_w_gate, ffn_w_up, ffn_w_down,
           moe_router, moe_w_gate, moe_w_up, moe_w_down):
    depth = w_in.shape[0]
    b, t, d = x_prompt.shape
    s = x_sample.shape[0]
    assert x_sample.shape[1] == 1 and s % 8 == 0 and b + s <= ADA_ROWS
    n_pages, page = page_table.shape[1], cache_k.shape[2]
    past_len = n_pages * page
    alpha = (2 * depth) ** 0.25
    sw = w_ba.shape[1]

    c_all = jnp.concatenate([c_prompt, c_sample, jnp.zeros((ADA_ROWS - b - s, d), F32)], axis=0)
    mod_all = _ada(c_all, w_ada, b_ada)

    tabs_p = _rope_tables(jnp.arange(t))
    tabs_s = _rope_tables(jnp.full((s,), past_len, jnp.int32))

    xp = x_prompt.reshape(b * t, d)
    xs = x_sample.reshape(s, d)
    k_p, v_p, k_s, v_s, vn_s = [], [], [], [], []
    aw = N_HEADS * HEAD_DIM
    q_lo = 2 * sw
    for l in range(depth):
        mod_p = mod_all[l, :b][:, None, :]
        mod_s = mod_all[l, b:b + s][None]
        w_in_b = w_in[l].astype(BF16)
        mix_w = (w_s[l], b_s[l], ln_v_g[l], ln_v_b[l], w_ba[l].astype(BF16), w_bb[l].astype(BF16),
                 w_o[l].astype(BF16), ln1_g[l], ln1_b[l])
        if l % 2 == 0:
            chan = functools.partial(
                _ffn, wg_b=ffn_w_gate[l // 2].astype(BF16), wu_b=ffn_w_up[l // 2].astype(BF16),
                wd_b=ffn_w_down[l // 2].astype(BF16), ln_g=ln2_g[l], ln_b=ln2_b[l], alpha=alpha)
        else:
            chan = functools.partial(
                _moe, router=moe_router[l // 2], wg_b=moe_w_gate[l // 2].astype(BF16),
                wu_b=moe_w_up[l // 2].astype(BF16), wd_b=moe_w_down[l // 2].astype(BF16),
                ln_g=ln2_g[l], ln_b=ln2_b[l], alpha=alpha)

        proj_p, k_new, v_new = _proj(xp, mod_p, w_in_b, tabs_p, decode=False, seq_len=t)
        attn_p = _moba_prefill(proj_p.reshape(b, t, -1)).reshape(b * t, -1)
        (xp1,) = _mix(proj_p, attn_p, xp, mod_p, *mix_w, decode=False, seq_len=t, alpha=alpha)
        xp = chan(xp1, mod_p, decode=False, seq_len=t)
        k_p.append(k_new.reshape(b, t, N_HEADS, HEAD_DIM))
        v_p.append(v_new.reshape(b, t, N_HEADS, HEAD_DIM))

        proj_s, k3, v3 = _proj(xs, mod_s, w_in_b, tabs_s, decode=True, seq_len=1)
        kmean = _paged_kmean(cache_k, page_table, l)
        sel = _dec_select(kmean, proj_s)
        q3 = proj_s[:, q_lo:q_lo + aw].reshape(s, N_HEADS, HEAD_DIM)
        attn_s = _dec_attn(cache_k, cache_v, page_table, sel, q3, k3, v3, l).reshape(s, aw)
        xs1, vn = _mix(proj_s, attn_s, xs, mod_s, *mix_w, decode=True, seq_len=1, alpha=alpha)
        xs = chan(xs1, mod_s, decode=True, seq_len=1)
        k_s.append(k3.reshape(s, 1, N_HEADS, HEAD_DIM))
        v_s.append(v3.reshape(s, 1, N_HEADS, HEAD_DIM))
        vn_s.append(vn.reshape(s, 1, sw))

    return (xp.reshape(b, t, d), xs.reshape(s, 1, d), jnp.stack(k_p), jnp.stack(v_p),
            jnp.stack(k_s), jnp.stack(v_s), jnp.stack(vn_s))
```

```python
import functools

import jax
import jax.numpy as jnp
from jax import lax
from jax.experimental import pallas as pl
from jax.experimental.pallas import tpu as pltpu

F32 = jnp.float32
BF16 = jnp.bfloat16

CHUNK = 128
SGU_GROUPS = 8
N_HEADS = 8
HEAD_DIM = 128
ROT_DIM = HEAD_DIM // 4
MOBA_BLOCK = 256
MOBA_TOPK = 3
ROPE_THETA = 500000.0
LN_EPS = 1e-5
N_EXPERTS = 8
LANES = 128
ADA_ROWS = 16
VMEM_LIMIT = 56 * 1024 * 1024
NEG_INF = float("-inf")

_NN = (((1,), (0,)), ((), ()))
_NT = (((1,), (1,)), ((), ()))


def _params(*sem):
    return pltpu.CompilerParams(dimension_semantics=sem, vmem_limit_bytes=VMEM_LIMIT)


def _sigmoid(x):
    return 1.0 / (1.0 + jnp.exp(-x))


def _gelu(x):
    return 0.5 * x * (1.0 + jnp.tanh(0.7978845608028654 * (x + 0.044715 * (x * x * x))))


def _layer_norm(x, g, b):
    mu = jnp.mean(x, axis=-1, keepdims=True)
    xc = x - mu
    var = jnp.mean(xc * xc, axis=-1, keepdims=True)
    return xc * lax.rsqrt(var + LN_EPS) * g + b


def _dot(a, b, dims=_NN):
    return lax.dot_general(a, b, dims, preferred_element_type=F32)


def _split_bf16(a):
    hi = a.astype(BF16)
    lo = (a - hi.astype(F32)).astype(BF16)
    return hi, lo


def _dot3(a, b, dims=_NN):
    ah, al = _split_bf16(a)
    bh, bl = _split_bf16(b)
    return _dot(ah, bh, dims) + (_dot(ah, bl, dims) + _dot(al, bh, dims))


def _ada_kernel(c_ref, w_ref, b_ref, o_ref):
    c = c_ref[...]
    a = (c * _sigmoid(c)).astype(BF16)
    o_ref[...] = _dot(a, w_ref[...].astype(BF16)) + b_ref[...]


def _ada(c_all, w_ada, b_ada):
    depth, d, n = w_ada.shape
    bn = 1024
    return pl.pallas_call(
        _ada_kernel,
        out_shape=jax.ShapeDtypeStruct((depth, ADA_ROWS, n), F32),
        grid=(depth, n // bn),
        in_specs=[
            pl.BlockSpec((ADA_ROWS, d), lambda l, j: (0, 0)),
            pl.BlockSpec((None, d, bn), lambda l, j: (l, 0, j)),
            pl.BlockSpec((None, 1, bn), lambda l, j: (l, 0, j)),
        ],
        out_specs=pl.BlockSpec((None, ADA_ROWS, bn), lambda l, j: (l, 0, j)),
        compiler_params=_params("arbitrary", "arbitrary"),
        name="ada",
    )(c_all, w_ada, b_ada.reshape(depth, 1, n))


def _mod_spec(mod3, chunk, d, decode, tiles_per_seq):
    rows = mod3.shape[1]
    if decode:
        return pl.BlockSpec((None, rows, d), lambda i, *_: (0, 0, chunk))
    return pl.BlockSpec((None, rows, d), lambda i, *_: (i // tiles_per_seq, 0, chunk))


def _rope_tables(pos):
    half = ROT_DIM // 2
    inv = ROPE_THETA ** (-jnp.arange(0, ROT_DIM, 2, dtype=F32) / ROT_DIM)
    ang = pos.astype(F32)[:, None] * inv[None, :]
    cos, sin = jnp.cos(ang), jnp.sin(ang)
    r = pos.shape[0]
    ones = jnp.ones((r, HEAD_DIM - ROT_DIM), F32)
    zeros = jnp.zeros((r, HEAD_DIM - ROT_DIM), F32)
    zhalf = jnp.zeros((r, half), F32)
    c = jnp.concatenate([cos, cos, ones], axis=1)
    s_lo = jnp.concatenate([zhalf, sin, zeros], axis=1)
    s_hi = jnp.concatenate([-sin, zhalf, zeros], axis=1)
    return c, s_lo, s_hi


PROJ_ROW_CHUNK = 128


def _proj_kernel(x_ref, sc_ref, sh_ref, w_ref, c_ref, slo_ref, shi_ref, o_ref, k_ref, v_ref, h_scr):
    j = pl.program_id(1)

    @pl.when(j == 0)
    def _():
        h_scr[...] = (x_ref[...] * (1.0 + sc_ref[...]) + sh_ref[...]).astype(BF16)

    rows, width = o_ref.shape
    chunk = min(rows, PROJ_ROW_CHUNK)

    def row_chunks(epilogue):
        for r in range(0, rows, chunk):
            rs = slice(r, r + chunk)
            epilogue(rs, _dot(h_scr[rs, :], w_ref[...]))

    def per_head(ref, rs, val):
        for h in range(N_HEADS):
            ref[rs, h, :] = val[:, h * HEAD_DIM:(h + 1) * HEAD_DIM]

    def rope(rs, acc):
        reps = width // HEAD_DIM
        c = jnp.concatenate([c_ref[rs, :]] * reps, axis=1)
        s_lo = jnp.concatenate([slo_ref[rs, :]] * reps, axis=1)
        s_hi = jnp.concatenate([shi_ref[rs, :]] * reps, axis=1)
        half = ROT_DIM // 2
        return acc * c + pltpu.roll(acc, half, 1) * s_lo + pltpu.roll(acc, width - half, 1) * s_hi

    def store(rs, val):
        o_ref[rs, :] = val

    def store_k(rs, acc):
        val = rope(rs, acc)
        o_ref[rs, :] = val
        per_head(k_ref, rs, val)

    def store_v(rs, acc):
        o_ref[rs, :] = acc
        per_head(v_ref, rs, acc)

    pl.when(j < 2)(lambda: row_chunks(lambda rs, acc: store(rs, _gelu(acc))))
    pl.when(j == 2)(lambda: row_chunks(lambda rs, acc: store(rs, rope(rs, acc))))
    pl.when(j == 3)(lambda: row_chunks(store_k))
    pl.when(j == 4)(lambda: row_chunks(store_v))
    pl.when(j > 4)(lambda: row_chunks(lambda rs, acc: store(rs, _sigmoid(acc))))


def _proj(x, mod3, w_in_b, tabs, *, decode, seq_len):
    m, d = x.shape
    n = w_in_b.shape[1]
    bn = 1024
    assert n == 9 * bn and d == 2 * bn
    bm = m if decode else min(512, seq_len)
    tps = 1 if decode else seq_len // bm
    if decode:
        tab_spec = pl.BlockSpec((bm, HEAD_DIM), lambda i, j: (0, 0))
    else:
        tab_spec = pl.BlockSpec((bm, HEAD_DIM), lambda i, j: (i % tps, 0))
    kv_shape = jax.ShapeDtypeStruct((m, N_HEADS, HEAD_DIM), F32)
    kv_spec = pl.BlockSpec((bm, N_HEADS, HEAD_DIM), lambda i, j: (i, 0, 0))
    return pl.pallas_call(
        _proj_kernel,
        out_shape=(jax.ShapeDtypeStruct((m, n), F32), kv_shape, kv_shape),
        grid=(m // bm, n // bn),
        in_specs=[
            pl.BlockSpec((bm, d), lambda i, j: (i, 0)),
            _mod_spec(mod3, 1, d, decode, tps),
            _mod_spec(mod3, 0, d, decode, tps),
            pl.BlockSpec((d, bn), lambda i, j: (0, j)),
            tab_spec, tab_spec, tab_spec,
        ],
        out_specs=(pl.BlockSpec((bm, bn), lambda i, j: (i, j)), kv_spec, kv_spec),
        scratch_shapes=[pltpu.VMEM((bm, d), BF16)],
        compiler_params=_params("arbitrary", "arbitrary"),
        name="proj_dec" if decode else "proj",
    )(x, mod3, mod3, w_in_b, *tabs)


MOBA_HEADS_PER_STEP = 2


MASKED = -1e30
LOG2_E = 1.4426950408889634


def _moba_kernel(q_ref, k_ref, v_ref, o_ref, ka_scr, qa_scr, vt_scr, km_scr, s_scr, *, nb, nbp, hp):
    qi = pl.program_id(2)
    blk = MOBA_BLOCK
    pair = 2 * blk
    t_len = nb * blk
    heads = range(hp)
    cols = [slice(h * HEAD_DIM, (h + 1) * HEAD_DIM) for h in heads]
    lane = lax.broadcasted_iota(jnp.int32, (blk, HEAD_DIM), 1)

    @pl.when(qi == 0)
    def _():
        ones_row = jnp.where(lax.broadcasted_iota(jnp.int32, (16, pair), 0) == 0, 1.0, 0.0).astype(BF16)
        for h in heads:
            for j in range(nb + 2):
                rows = slice(j * blk, (j + 1) * blk)
                if j < nb:
                    kj = k_ref[rows, cols[h]]
                    km_scr[h, j:j + 1, :] = jnp.sum(kj, axis=0, keepdims=True) * (1.0 / blk)
                    ka_scr[h, rows, 0:HEAD_DIM] = kj.astype(BF16)
                else:
                    ka_scr[h, rows, 0:HEAD_DIM] = jnp.zeros((blk, HEAD_DIM), BF16)
                ka_scr[h, rows, HEAD_DIM:] = jnp.where(lane == min(j, nb), 1.0, 0.0).astype(BF16)
            for j in range(nb // 2):
                vt_scr[h, j, 0:HEAD_DIM, :] = v_ref[j * pair:(j + 1) * pair, cols[h]].T.astype(BF16)
                vt_scr[h, j, HEAD_DIM:, :] = ones_row
            qa_scr[h, HEAD_DIM + nbp:, :] = jnp.zeros((HEAD_DIM - nbp, blk), BF16)

    n_idx = lax.broadcasted_iota(jnp.int32, (nb, blk), 0)
    for h in heads:
        q = q_ref[:, cols[h]]
        qa_scr[h, 0:HEAD_DIM, :] = (q * (HEAD_DIM ** -0.5 * LOG2_E)).T.astype(BF16)
        score = _dot3(km_scr[h], (q * (HEAD_DIM ** -0.5)).T)
        cnt = jnp.zeros((nb, blk), F32)
        for m in range(nb):
            row = score[m:m + 1, :]
            beats = jnp.where(row > score, 1.0, jnp.where(row == score, jnp.where(m < n_idx, 1.0, 0.0), 0.0))
            cnt = cnt + jnp.where(m < qi, beats, 0.0)
        past = jnp.where(cnt < MOBA_TOPK, 0.0, MASKED)
        bias = jnp.where(n_idx < qi, past, jnp.where(n_idx == qi, 0.0, MASKED))
        bias = jnp.concatenate([bias, jnp.full((nbp - nb, blk), MASKED, F32)], axis=0)
        qa_scr[h, HEAD_DIM:HEAD_DIM + nbp, :] = bias.astype(BF16)

    def pair_scores(h, t):
        return _dot(ka_scr[h, pl.ds(pl.multiple_of(t * pair, pair), pair), :], qa_scr[h])

    def online_step(state, h, slot, t):
        m_run, acc = state
        s = s_scr[slot, h]
        m_new = jnp.maximum(m_run, jnp.max(s, axis=0, keepdims=True))
        p = jnp.exp2(s - m_new).astype(BF16)
        return [m_new, jnp.exp2(m_run - m_new) * acc + _dot(vt_scr[h, t], p)]

    t_own = lax.shift_right_logical(qi, 1)
    last = jnp.maximum(t_own - 1, 0)
    rel = lax.broadcasted_iota(jnp.int32, (pair, blk), 0) - (qi - 2 * t_own) * blk
    qry_i = lax.broadcasted_iota(jnp.int32, (pair, blk), 1)
    for h in heads:
        s = pair_scores(h, t_own)
        s_scr[0, h] = jnp.where(rel > qry_i, jnp.where(rel < blk, MASKED, s), s)

    def body(u, carry):
        t1 = 2 * u
        k1 = jnp.where(t1 < t_own, t1, nb // 2)
        v0 = jnp.where(u == 0, t_own, t1 - 1)
        v1 = jnp.minimum(t1, last)
        k2 = jnp.minimum(t1 + 1, last)
        state = [list(carry[2 * h:2 * h + 2]) for h in heads]
        for h in heads:
            s_scr[1, h] = pair_scores(h, k1)
        for h in heads:
            state[h] = online_step(state[h], h, 0, v0)
        for h in heads:
            s_scr[0, h] = pair_scores(h, k2)
        for h in heads:
            state[h] = online_step(state[h], h, 1, v1)
        return tuple(x for st in state for x in st)

    init = [jnp.full((1, blk), MASKED, F32), jnp.zeros((HEAD_DIM + 16, blk), F32)] * hp
    fin = lax.fori_loop(0, lax.shift_right_logical(t_own + 2, 1), body, tuple(init))
    for h in heads:
        acc = fin[2 * h + 1]
        o_ref[:, cols[h]] = (acc[0:HEAD_DIM, :] / acc[HEAD_DIM:HEAD_DIM + 1, :]).T


def _moba_prefill(proj3):
    b, t, _ = proj3.shape
    blk = MOBA_BLOCK
    nb = t // blk
    hp = MOBA_HEADS_PER_STEP
    assert t % blk == 0 and nb % 8 == 0 and N_HEADS % hp == 0
    w = hp * HEAD_DIM
    q_col, k_col, v_col = (c * HEAD_DIM // w for c in (16, 24, 32))
    nbp = -(-(nb + 1) // 16) * 16
    assert nbp <= HEAD_DIM
    return pl.pallas_call(
        functools.partial(_moba_kernel, nb=nb, nbp=nbp, hp=hp),
        out_shape=jax.ShapeDtypeStruct((b, t, N_HEADS * HEAD_DIM), F32),
        grid=(b, N_HEADS // hp, nb),
        in_specs=[
            pl.BlockSpec((None, blk, w), lambda bi, h, qi: (bi, qi, q_col + h)),
            pl.BlockSpec((None, t, w), lambda bi, h, qi: (bi, 0, k_col + h)),
            pl.BlockSpec((None, t, w), lambda bi, h, qi: (bi, 0, v_col + h)),
        ],
        out_specs=pl.BlockSpec((None, blk, w), lambda bi, h, qi: (bi, qi, h)),
        scratch_shapes=[
            pltpu.VMEM((hp, t + 2 * blk, 2 * HEAD_DIM), BF16),
            pltpu.VMEM((hp, 2 * HEAD_DIM, blk), BF16),
            pltpu.VMEM((hp, nb // 2, HEAD_DIM + 16, 2 * blk), BF16),
            pltpu.VMEM((hp, nb, HEAD_DIM), F32),
            pltpu.VMEM((2, hp, 2 * blk, blk), F32),
        ],
        compiler_params=_params("arbitrary", "arbitrary", "arbitrary"),
        name="moba",
    )(proj3, proj3, proj3)


def _paged_kmean_kernel(pt_ref, *refs, ppb):
    del pt_ref
    pages, o_ref = refs[:ppb], refs[ppb]
    for b in range(ppb // 2):
        tot = jnp.sum(pages[2 * b][...], axis=0) + jnp.sum(pages[2 * b + 1][...], axis=0)
        o_ref[b] = tot * (1.0 / MOBA_BLOCK)


def _paged_kmean(cache_k, page_table, layer):
    _, _, page, nh, dh = cache_k.shape
    s, n_pages = page_table.shape
    assert 2 * page == MOBA_BLOCK
    ppb = 16 if n_pages % 16 == 0 else 2
    page_specs = [
        pl.BlockSpec((None, None, page, nh, dh),
                     lambda si, c, pt, i=i: (layer, pt[si, c * ppb + i], 0, 0, 0))
        for i in range(ppb)
    ]
    return pl.pallas_call(
        functools.partial(_paged_kmean_kernel, ppb=ppb),
        out_shape=jax.ShapeDtypeStruct((s, n_pages // 2, nh, dh), F32),
        grid_spec=pltpu.PrefetchScalarGridSpec(
            num_scalar_prefetch=1,
            grid=(s, n_pages // ppb),
            in_specs=page_specs,
            out_specs=pl.BlockSpec((None, ppb // 2, nh, dh), lambda si, c, pt: (si, c, 0, 0)),
        ),
        compiler_params=_params("arbitrary", "arbitrary"),
        name="paged_kmean",
    )(page_table, *([cache_k] * ppb))


def _dec_select_kernel(km_ref, q_ref, o_ref, *, n_blocks):
    si = pl.program_id(0)
    q = q_ref[pl.ds(si, 1), :] * (HEAD_DIM ** -0.5)
    lane = lax.broadcasted_iota(jnp.int32, (8, LANES), 1).astype(F32)
    for h in range(N_HEADS):
        qh = jnp.broadcast_to(q[:, h * HEAD_DIM:(h + 1) * HEAD_DIM], (8, HEAD_DIM))
        sc = _dot3(qh, km_ref[h], _NT)
        sc = jnp.where(lane < n_blocks, sc, NEG_INF)
        row = jnp.zeros((8, LANES), F32)
        for r in range(MOBA_TOPK):
            best = jnp.max(sc, axis=1, keepdims=True)
            idx = jnp.min(jnp.where(sc == best, lane, 1e9), axis=1, keepdims=True)
            row = jnp.where(lane == r, idx, row)
            sc = jnp.where(lane == idx, NEG_INF, sc)
        o_ref[pl.ds(h, 1), :] = row[0:1, :].astype(jnp.int32)


def _dec_select(kmean, proj_s):
    s, n_blocks, nh, dh = kmean.shape
    assert MOBA_TOPK <= n_blocks <= LANES
    km = jnp.transpose(kmean, (0, 2, 1, 3))
    km = jnp.pad(km, ((0, 0), (0, 0), (0, LANES - n_blocks), (0, 0)))
    return pl.pallas_call(
        functools.partial(_dec_select_kernel, n_blocks=n_blocks),
        out_shape=jax.ShapeDtypeStruct((s, nh, LANES), jnp.int32),
        grid=(s,),
        in_specs=[
            pl.BlockSpec((None, nh, LANES, dh), lambda si: (si, 0, 0, 0)),
            pl.BlockSpec((s, nh * dh), lambda si: (0, 2)),
        ],
        out_specs=pl.BlockSpec((None, nh, LANES), lambda si: (si, 0, 0)),
        compiler_params=_params("arbitrary"),
        name="dec_select",
    )(km, proj_s)


def _dec_attn_kernel(sel_ref, pt_ref, *refs):
    del sel_ref, pt_ref
    n_sel = 2 * MOBA_TOPK
    k_refs, v_refs = refs[:n_sel], refs[n_sel:2 * n_sel]
    q_ref, kn_ref, vn_ref, o_ref = refs[2 * n_sel:]
    h = pl.program_id(1)
    q = q_ref[...] * (HEAD_DIM ** -0.5)
    is_h = lax.broadcasted_iota(jnp.int32, (N_HEADS, 1), 0) == h
    s_sel = []
    m = jnp.where(is_h, jnp.sum(q * kn_ref[...], axis=-1, keepdims=True), NEG_INF)
    s_own = m
    for i in range(n_sel):
        s_i = jnp.sum(k_refs[i][...] * q[None], axis=-1, keepdims=True)
        s_i = jnp.where(is_h[None], s_i, NEG_INF)
        s_sel.append(s_i)
        m = jnp.maximum(m, jnp.max(s_i, axis=0))
    m = jnp.max(m, axis=0, keepdims=True)
    p_own = jnp.exp(s_own - m)
    denom = p_own
    out = p_own * vn_ref[...]
    for i in range(n_sel):
        p = jnp.exp(s_sel[i] - m)
        denom = denom + jnp.sum(p, axis=0)
        out = out + jnp.sum(p * v_refs[i][...], axis=0)
    out = jnp.sum(out, axis=0, keepdims=True) / jnp.sum(denom, axis=0, keepdims=True)
    o_ref[pl.ds(h, 1), :] = out


def _dec_attn(cache_k, cache_v, page_table, sel, q3, k3, v3, layer):
    _, _, page, nh, dh = cache_k.shape
    s = page_table.shape[0]
    sel_flat = sel[:, :, :MOBA_TOPK].reshape(-1)
    n_sel = 2 * MOBA_TOPK

    def page_spec(i):
        j, r = divmod(i, 2)

        def index_map(si, h, sel_r, pt_r):
            blk = sel_r[(si * nh + h) * MOBA_TOPK + j]
            return (layer, pt_r[si, 2 * blk + r], 0, 0, 0)

        return pl.BlockSpec((None, None, page, nh, dh), index_map)

    row_spec = pl.BlockSpec((None, nh, dh), lambda si, h, sel_r, pt_r: (si, 0, 0))
    return pl.pallas_call(
        _dec_attn_kernel,
        out_shape=jax.ShapeDtypeStruct((s, nh, dh), F32),
        grid_spec=pltpu.PrefetchScalarGridSpec(
            num_scalar_prefetch=2,
            grid=(s, nh),
            in_specs=[page_spec(i) for i in range(n_sel)] * 2 + [row_spec] * 3,
            out_specs=row_spec,
        ),
        compiler_params=_params("arbitrary", "arbitrary"),
        name="dec_attn",
    )(sel_flat, page_table, *([cache_k] * n_sel), *([cache_v] * n_sel), q3, k3, v3)


def _mix_kernel(u_ref, va_ref, attn_ref, ga0_ref, ga1_ref, gb0_ref, gb1_ref, x_ref, g1_ref,
                ws_ref, bs_ref, lvg_ref, lvb_ref, wba_ref, wbb_ref, wo_ref, l1g_ref, l1b_ref,
                *out_refs, decode, alpha):
    vn = _layer_norm(va_ref[...], lvg_ref[...], lvb_ref[...])
    if decode:
        out_refs[1][...] = vn
        mixed = vn * ws_ref[...] + bs_ref[...]
    else:
        rows = vn.shape[0]
        vnb = vn.astype(BF16)
        r_i = lax.broadcasted_iota(jnp.int32, (CHUNK, CHUNK), 0)
        c_i = lax.broadcasted_iota(jnp.int32, (CHUNK, CHUNK), 1)
        w_tril = [jnp.where(c_i <= r_i, ws_ref[g], 0.0).astype(BF16) for g in range(SGU_GROUPS)]
        gd = vn.shape[1] // SGU_GROUPS
        chunks = []
        for c in range(rows // CHUNK):
            cols = []
            for g in range(SGU_GROUPS):
                v_cg = vnb[c * CHUNK:(c + 1) * CHUNK, g * gd:(g + 1) * gd]
                cols.append(_dot(w_tril[g], v_cg) + bs_ref[:, g:g + 1])
            chunks.append(jnp.concatenate(cols, axis=1))
        mixed = jnp.concatenate(chunks, axis=0)
    a_pre = (u_ref[...] * mixed).astype(BF16)
    a_out = _dot(a_pre, wba_ref[...])
    b_out = _dot(attn_ref[...].astype(BF16), wbb_ref[...])
    g_a = jnp.concatenate([ga0_ref[...], ga1_ref[...]], axis=1)
    g_b = jnp.concatenate([gb0_ref[...], gb1_ref[...]], axis=1)
    merged = (g_a * a_out + g_b * b_out).astype(BF16)
    y = _dot(merged, wo_ref[...])
    out_refs[0][...] = _layer_norm(alpha * x_ref[...] + (1.0 + g1_ref[...]) * y, l1g_ref[...], l1b_ref[...])


def _mix(proj, attn, x, mod3, w_s, b_s, ln_v_g, ln_v_b, w_ba_b, w_bb_b, w_o_b, ln1_g, ln1_b,
         *, decode, seq_len, alpha):
    m, d = x.shape
    sw = w_ba_b.shape[0]
    bm = m if decode else min(256, seq_len)
    tps = 1 if decode else seq_len // bm
    const = lambda *_: (0, 0)
    single = pl.Buffered(1)

    def col(cidx):
        return pl.BlockSpec((bm, sw), lambda i: (i, cidx))

    if decode:
        gd = sw // SGU_GROUPS
        ws_arg = jnp.repeat(w_s[:, 0, 0], gd)[None, :]
        bs_arg = jnp.repeat(b_s[:, 0], gd)[None, :]
        ws_spec = pl.BlockSpec((1, sw), const)
        bs_spec = pl.BlockSpec((1, sw), const)
        out_shape = (jax.ShapeDtypeStruct((m, d), F32), jax.ShapeDtypeStruct((m, sw), F32))
        out_specs = (pl.BlockSpec((bm, d), lambda i: (i, 0)), pl.BlockSpec((bm, sw), lambda i: (i, 0)))
    else:
        ws_arg, bs_arg = w_s, b_s.T
        ws_spec = pl.BlockSpec(w_s.shape, lambda i: (0, 0, 0))
        bs_spec = pl.BlockSpec((CHUNK, SGU_GROUPS), const)
        out_shape = (jax.ShapeDtypeStruct((m, d), F32),)
        out_specs = (pl.BlockSpec((bm, d), lambda i: (i, 0)),)

    return pl.pallas_call(
        functools.partial(_mix_kernel, decode=decode, alpha=alpha),
        out_shape=out_shape,
        grid=(m // bm,),
        in_specs=[
            col(0), col(1),
            pl.BlockSpec((bm, sw), lambda i: (i, 0)),
            col(5), col(6), col(7), col(8),
            pl.BlockSpec((bm, d), lambda i: (i, 0)),
            _mod_spec(mod3, 2, d, decode, tps),
            ws_spec, bs_spec,
            pl.BlockSpec((1, sw), const), pl.BlockSpec((1, sw), const),
            pl.BlockSpec((sw, d), const, pipeline_mode=single),
            pl.BlockSpec((sw, d), const, pipeline_mode=single),
            pl.BlockSpec((d, d), const, pipeline_mode=single),
            pl.BlockSpec((1, d), const), pl.BlockSpec((1, d), const),
        ],
        out_specs=out_specs,
        compiler_params=_params("arbitrary"),
        name="mix_dec" if decode else "mix",
    )(proj, proj, attn, proj, proj, proj, proj, x, mod3, ws_arg, bs_arg,
      ln_v_g[None, :], ln_v_b[None, :], w_ba_b, w_bb_b, w_o_b, ln1_g[None, :], ln1_b[None, :])


def _ffn_kernel(x_ref, sc_ref, sh_ref, g_ref, wg_ref, wu_ref, wd_ref, lg_ref, lb_ref, o_ref,
                h_scr, acc_scr, *, alpha):
    f = pl.program_id(1)

    @pl.when(f == 0)
    def _():
        h_scr[...] = (x_ref[...] * (1.0 + sc_ref[...]) + sh_ref[...]).astype(BF16)
        acc_scr[...] = jnp.zeros_like(acc_scr)

    h = h_scr[...]
    a = _dot(h, wg_ref[...])
    act = (a * _sigmoid(a) * _dot(h, wu_ref[...])).astype(BF16)
    acc_scr[...] += _dot(act, wd_ref[...])

    @pl.when(f == pl.num_programs(1) - 1)
    def _():
        y = alpha * x_ref[...] + (1.0 + g_ref[...]) * acc_scr[...]
        o_ref[...] = _layer_norm(y, lg_ref[...], lb_ref[...])


def _ffn(x, mod3, wg_b, wu_b, wd_b, ln_g, ln_b, *, decode, seq_len, alpha):
    m, d = x.shape
    ff = wg_b.shape[1]
    bf = 512 if ff % 512 == 0 else ff
    bm = m if decode else min(512, seq_len)
    tps = 1 if decode else seq_len // bm
    const = lambda *_: (0, 0)
    return pl.pallas_call(
        functools.partial(_ffn_kernel, alpha=alpha),
        out_shape=jax.ShapeDtypeStruct((m, d), F32),
        grid=(m // bm, ff // bf),
        in_specs=[
            pl.BlockSpec((bm, d), lambda i, f: (i, 0)),
            _mod_spec(mod3, 4, d, decode, tps),
            _mod_spec(mod3, 3, d, decode, tps),
            _mod_spec(mod3, 5, d, decode, tps),
            pl.BlockSpec((d, bf), lambda i, f: (0, f)),
            pl.BlockSpec((d, bf), lambda i, f: (0, f)),
            pl.BlockSpec((bf, d), lambda i, f: (f, 0)),
            pl.BlockSpec((1, d), const), pl.BlockSpec((1, d), const),
        ],
        out_specs=pl.BlockSpec((bm, d), lambda i, f: (i, 0)),
        scratch_shapes=[pltpu.VMEM((bm, d), BF16), pltpu.VMEM((bm, d), F32)],
        compiler_params=_params("arbitrary", "arbitrary"),
        name="ffn_dec" if decode else "ffn",
    )(x, mod3, mod3, mod3, wg_b, wu_b, wd_b, ln_g[None, :], ln_b[None, :])


def _moe_route_kernel(x_ref, sc_ref, sh_ref, r_ref, h_ref, route_ref, cnt_ref, cnt_scr):
    i = pl.program_id(0)
    rows = x_ref.shape[0]

    @pl.when(i == 0)
    def _():
        cnt_scr[...] = jnp.zeros_like(cnt_scr)

    h = x_ref[...] * (1.0 + sc_ref[...]) + sh_ref[...]
    h_ref[...] = h
    lane = lax.broadcasted_iota(jnp.int32, (rows, LANES), 1)
    lane_f = lane.astype(F32)
    logits = jnp.where(lane < N_EXPERTS, _dot3(h, r_ref[...]), NEG_INF)
    v1 = jnp.max(logits, axis=1, keepdims=True)
    i1 = jnp.min(jnp.where(logits == v1, lane_f, 1e9), axis=1, keepdims=True)
    rest = jnp.where(lane_f == i1, NEG_INF, logits)
    v2 = jnp.max(rest, axis=1, keepdims=True)
    i2 = jnp.min(jnp.where(rest == v2, lane_f, 1e9), axis=1, keepdims=True)
    ex = jnp.exp(v2 - v1)
    w1 = 1.0 / (1.0 + ex)
    w2 = ex / (1.0 + ex)
    picks = jnp.where(lane_f == i1, 1.0, jnp.where(lane_f == i2, 1.0, 0.0))
    if rows >= LANES:
        r_i = lax.broadcasted_iota(jnp.int32, (rows, rows), 0)
        c_i = lax.broadcasted_iota(jnp.int32, (rows, rows), 1)
        before = _dot(jnp.where(c_i < r_i, 1.0, 0.0).astype(BF16), picks.astype(BF16))
    else:
        r_i = lax.broadcasted_iota(jnp.int32, (rows, LANES), 0)
        before = jnp.zeros((rows, LANES), F32)
        for t in range(rows - 1):
            before = before + jnp.where(r_i > t, picks[t:t + 1, :], 0.0)
    before = before + cnt_scr[0:1, :]
    rank1 = jnp.sum(jnp.where(lane_f == i1, before, 0.0), axis=1, keepdims=True)
    rank2 = jnp.sum(jnp.where(lane_f == i2, before, 0.0), axis=1, keepdims=True)
    rec = jnp.zeros((rows, LANES), F32)
    for k, val in enumerate((i1, i2, w1, w2, rank1, rank2)):
        rec = jnp.where(lane == k, val, rec)
    route_ref[...] = rec
    cnt_scr[...] = cnt_scr[...] + jnp.sum(picks, axis=0, keepdims=True)
    cnt_ref[...] = cnt_scr[...]


def _moe_route(x, mod3, router, *, decode, seq_len):
    m, d = x.shape
    bm = m if decode else min(512, seq_len)
    tps = 1 if decode else seq_len // bm
    router_p = jnp.pad(router, ((0, 0), (0, LANES - router.shape[1])))
    return pl.pallas_call(
        _moe_route_kernel,
        out_shape=(jax.ShapeDtypeStruct((m, d), F32), jax.ShapeDtypeStruct((m, LANES), F32),
                   jax.ShapeDtypeStruct((8, LANES), F32)),
        grid=(m // bm,),
        in_specs=[
            pl.BlockSpec((bm, d), lambda i: (i, 0)),
            _mod_spec(mod3, 4, d, decode, tps),
            _mod_spec(mod3, 3, d, decode, tps),
            pl.BlockSpec((d, LANES), lambda i: (0, 0)),
        ],
        out_specs=(pl.BlockSpec((bm, d), lambda i: (i, 0)), pl.BlockSpec((bm, LANES), lambda i: (i, 0)),
                   pl.BlockSpec((8, LANES), lambda i: (0, 0))),
        scratch_shapes=[pltpu.VMEM((8, LANES), F32)],
        compiler_params=_params("arbitrary"),
        name="moe_route_dec" if decode else "moe_route",
    )(x, mod3, mod3, router_p)


def _row_gather_start(src_hbm, idx_ref, base, dst, sem):
    def issue(r, carry):
        pltpu.make_async_copy(src_hbm.at[pl.ds(idx_ref[base + r], 1), :], dst.at[pl.ds(r, 1), :], sem).start()
        return carry

    lax.fori_loop(0, dst.shape[0], issue, 0)


def _row_gather_wait(src_hbm, dst, sem):
    pltpu.make_async_copy(src_hbm.at[pl.ds(0, dst.shape[0]), :], dst, sem).wait()


def _moe_group_kernel(te_ref, tv_ref, tok_ref, h_hbm, wg_ref, wu_ref, wd_ref, y_ref,
                      xg_scr, hb_scr, acc_scr, sem):
    del te_ref
    i = pl.program_id(0)
    f = pl.program_id(1)
    rows = xg_scr.shape[0]
    valid = tv_ref[i] == 1
    last = f == pl.num_programs(1) - 1

    @pl.when(valid & (f == 0))
    def _():
        _row_gather_start(h_hbm, tok_ref, i * rows, xg_scr, sem)
        _row_gather_wait(h_hbm, xg_scr, sem)
        hb_scr[...] = xg_scr[...].astype(BF16)
        acc_scr[...] = jnp.zeros_like(acc_scr)

    @pl.when(valid)
    def _():
        h = hb_scr[...]
        a = _dot(h, wg_ref[...])
        act = (a * _sigmoid(a) * _dot(h, wu_ref[...])).astype(BF16)
        acc_scr[...] += _dot(act, wd_ref[...])

    @pl.when(valid & last)
    def _():
        y_ref[...] = acc_scr[...]

    @pl.when(jnp.logical_not(valid) & last)
    def _():
        y_ref[...] = jnp.zeros_like(y_ref)


def _moe_group(h2, tile_expert, tile_valid, tok_of_slot, wg_b, wu_b, wd_b, *, rows, decode):
    d = h2.shape[1]
    ff = wg_b.shape[2]
    bf = 512 if ff % 512 == 0 else ff
    nf = ff // bf
    n_tiles = tile_expert.shape[0]

    def f_idx(i, f, tv):
        return jnp.where(tv[i] == 1, f, nf - 1)

    return pl.pallas_call(
        _moe_group_kernel,
        out_shape=jax.ShapeDtypeStruct((n_tiles * rows, d), F32),
        grid_spec=pltpu.PrefetchScalarGridSpec(
            num_scalar_prefetch=3,
            grid=(n_tiles, nf),
            in_specs=[
                pl.BlockSpec(memory_space=pl.ANY),
                pl.BlockSpec((None, d, bf), lambda i, f, te, tv, tok: (te[i], 0, f_idx(i, f, tv))),
                pl.BlockSpec((None, d, bf), lambda i, f, te, tv, tok: (te[i], 0, f_idx(i, f, tv))),
                pl.BlockSpec((None, bf, d), lambda i, f, te, tv, tok: (te[i], f_idx(i, f, tv), 0)),
            ],
            out_specs=pl.BlockSpec((rows, d), lambda i, f, te, tv, tok: (i, 0)),
            scratch_shapes=[pltpu.VMEM((rows, d), F32), pltpu.VMEM((rows, d), BF16),
                            pltpu.VMEM((rows, d), F32), pltpu.SemaphoreType.DMA],
        ),
        compiler_params=_params("arbitrary", "arbitrary"),
        name="moe_group_dec" if decode else "moe_group",
    )(tile_expert, tile_valid, tok_of_slot, h2, wg_b, wu_b, wd_b)


def _moe_combine_kernel(s1_ref, s2_ref, y_hbm, route_ref, x_ref, g_ref, lg_ref, lb_ref, o_ref,
                        y1_scr, y2_scr, sem1, sem2, *, alpha):
    i = pl.program_id(0)
    rows = x_ref.shape[0]
    _row_gather_start(y_hbm, s1_ref, i * rows, y1_scr, sem1)
    _row_gather_start(y_hbm, s2_ref, i * rows, y2_scr, sem2)
    _row_gather_wait(y_hbm, y1_scr, sem1)
    _row_gather_wait(y_hbm, y2_scr, sem2)
    route = route_ref[...]
    f = route[:, 2:3] * y1_scr[...] + route[:, 3:4] * y2_scr[...]
    y = alpha * x_ref[...] + (1.0 + g_ref[...]) * f
    o_ref[...] = _layer_norm(y, lg_ref[...], lb_ref[...])


def _moe_combine(y_sorted, slot1, slot2, route, x, mod3, ln_g, ln_b, *, decode, seq_len, alpha):
    m, d = x.shape
    bm = m if decode else min(256, seq_len)
    tps = 1 if decode else seq_len // bm
    rows_mod = mod3.shape[1]
    if decode:
        g_spec = pl.BlockSpec((None, rows_mod, d), lambda i, s1, s2: (0, 0, 5))
    else:
        g_spec = pl.BlockSpec((None, rows_mod, d), lambda i, s1, s2: (i // tps, 0, 5))
    const = lambda i, s1, s2: (0, 0)
    return pl.pallas_call(
        functools.partial(_moe_combine_kernel, alpha=alpha),
        out_shape=jax.ShapeDtypeStruct((m, d), F32),
        grid_spec=pltpu.PrefetchScalarGridSpec(
            num_scalar_prefetch=2,
            grid=(m // bm,),
            in_specs=[
                pl.BlockSpec(memory_space=pl.ANY),
                pl.BlockSpec((bm, LANES), lambda i, s1, s2: (i, 0)),
                pl.BlockSpec((bm, d), lambda i, s1, s2: (i, 0)),
                g_spec,
                pl.BlockSpec((1, d), const), pl.BlockSpec((1, d), const),
            ],
            out_specs=pl.BlockSpec((bm, d), lambda i, s1, s2: (i, 0)),
            scratch_shapes=[pltpu.VMEM((bm, d), F32), pltpu.VMEM((bm, d), F32),
                            pltpu.SemaphoreType.DMA, pltpu.SemaphoreType.DMA],
        ),
        compiler_params=_params("arbitrary"),
        name="moe_combine_dec" if decode else "moe_combine",
    )(slot1, slot2, y_sorted, route, x, mod3, ln_g[None, :], ln_b[None, :])


def _moe(x, mod3, router, wg_b, wu_b, wd_b, ln_g, ln_b, *, decode, seq_len, alpha):
    m = x.shape[0]
    ne = wg_b.shape[0]
    rows = 16 if decode else 512
    n_tiles = -(-2 * m // rows) + ne
    h2, route, counts = _moe_route(x, mod3, router, decode=decode, seq_len=seq_len)

    cnt = counts[0, :ne].astype(jnp.int32)
    padded = (cnt + rows - 1) // rows * rows
    ends = jnp.cumsum(padded)
    starts = ends - padded
    e1, e2 = route[:, 0].astype(jnp.int32), route[:, 1].astype(jnp.int32)
    slot1 = starts[e1] + route[:, 4].astype(jnp.int32)
    slot2 = starts[e2] + route[:, 5].astype(jnp.int32)
    tok = jnp.arange(m, dtype=jnp.int32)
    tok_of_slot = jnp.zeros((n_tiles * rows,), jnp.int32).at[slot1].set(tok).at[slot2].set(tok)
    tile_lo = jnp.arange(n_tiles, dtype=jnp.int32) * rows
    tile_valid = (tile_lo < ends[-1]).astype(jnp.int32)
    last_lo = jnp.maximum(ends[-1] - rows, 0)
    tile_expert = jnp.sum((jnp.minimum(tile_lo, last_lo)[:, None] >= ends[None, :]).astype(jnp.int32), axis=1)
    tile_expert = jnp.minimum(tile_expert, ne - 1)

    y_sorted = _moe_group(h2, tile_expert, tile_valid, tok_of_slot, wg_b, wu_b, wd_b, rows=rows, decode=decode)
    return _moe_combine(y_sorted, slot1, slot2, route, x, mod3, ln_g, ln_b,
                        decode=decode, seq_len=seq_len, alpha=alpha)


def kernel(x_prompt, x_sample, cache_k, cache_v, page_table, c_prompt, c_sample,
           w_in, ln_v_g, ln_v_b, w_s, b_s, w_ba, w_bb, w_o, w_ada, b_ada,
           ln1_g, ln1_b, ln2_g, ln2_b, ffn_w_gate, ffn_w_up, ffn_w_down,
           moe_router, moe_w_gate, moe_w_up, moe_w_down):
    depth = w_in.shape[0]
    b, t, d = x_prompt.shape
    s = x_sample.shape[0]
    assert x_sample.shape[1] == 1 and s % 8 == 0 and b + s <= ADA_ROWS
    n_pages, page = page_table.shape[1], cache_k.shape[2]
    past_len = n_pages * page
    alpha = (2 * depth) ** 0.25
    sw = w_ba.shape[1]

    c_all = jnp.concatenate([c_prompt, c_sample, jnp.zeros((ADA_ROWS - b - s, d), F32)], axis=0)
    mod_all = _ada(c_all, w_ada, b_ada)

    tabs_p = _rope_tables(jnp.arange(t))
    tabs_s = _rope_tables(jnp.full((s,), past_len, jnp.int32))

    xp = x_prompt.reshape(b * t, d)
    xs = x_sample.reshape(s, d)
    k_p, v_p, k_s, v_s, vn_s = [], [], [], [], []
    aw = N_HEADS * HEAD_DIM
    q_lo = 2 * sw
    for l in range(depth):
        mod_p = mod_all[l, :b][:, None, :]
        mod_s = mod_all[l, b:b + s][None]
        w_in_b = w_in[l].astype(BF16)
        mix_w = (w_s[l], b_s[l], ln_v_g[l], ln_v_b[l], w_ba[l].astype(BF16), w_bb[l].astype(BF16),
                 w_o[l].astype(BF16), ln1_g[l], ln1_b[l])
        if l % 2 == 0:
            chan = functools.partial(
                _ffn, wg_b=ffn_w_gate[l // 2].astype(BF16), wu_b=ffn_w_up[l // 2].astype(BF16),
                wd_b=ffn_w_down[l // 2].astype(BF16), ln_g=ln2_g[l], ln_b=ln2_b[l], alpha=alpha)
        else:
            chan = functools.partial(
                _moe, router=moe_router[l // 2], wg_b=moe_w_gate[l // 2].astype(BF16),
                wu_b=moe_w_up[l // 2].astype(BF16), wd_b=moe_w_down[l // 2].astype(BF16),
                ln_g=ln2_g[l], ln_b=ln2_b[l], alpha=alpha)

        proj_p, k_new, v_new = _proj(xp, mod_p, w_in_b, tabs_p, decode=False, seq_len=t)
        attn_p = _moba_prefill(proj_p.reshape(b, t, -1)).reshape(b * t, -1)
        (xp1,) = _mix(proj_p, attn_p, xp, mod_p, *mix_w, decode=False, seq_len=t, alpha=alpha)
        xp = chan(xp1, mod_p, decode=False, seq_len=t)
        k_p.append(k_new.reshape(b, t, N_HEADS, HEAD_DIM))
        v_p.append(v_new.reshape(b, t, N_HEADS, HEAD_DIM))

        proj_s, k3, v3 = _proj(xs, mod_s, w_in_b, tabs_s, decode=True, seq_len=1)
        kmean = _paged_kmean(cache_k, page_table, l)
        sel = _dec_select(kmean, proj_s)
        q3 = proj_s[:, q_lo:q_lo + aw].reshape(s, N_HEADS, HEAD_DIM)
        attn_s = _dec_attn(cache_k, cache_v, page_table, sel, q3, k3, v3, l).reshape(s, aw)
        xs1, vn = _mix(proj_s, attn_s, xs, mod_s, *mix_w, decode=True, seq_len=1, alpha=alpha)
        xs = chan(xs1, mod_s, decode=True, seq_len=1)
        k_s.append(k3.reshape(s, 1, N_HEADS, HEAD_DIM))
        v_s.append(v3.reshape(s, 1, N_HEADS, HEAD_DIM))
        vn_s.append(vn.reshape(s, 1, sw))

    return (xp.reshape(b, t, d), xs.reshape(s, 1, d), jnp.stack(k_p), jnp.stack(v_p),
            jnp.stack(k_s), jnp.stack(v_s), jnp.stack(vn_s))
```

```python
import functools

import jax
import jax.numpy as jnp
from jax import lax
from jax.experimental import pallas as pl
from jax.experimental.pallas import tpu as pltpu

F32 = jnp.float32
BF16 = jnp.bfloat16

CHUNK = 128
SGU_GROUPS = 8
N_HEADS = 8
HEAD_DIM = 128
ROT_DIM = HEAD_DIM // 4
MOBA_BLOCK = 256
MOBA_TOPK = 3
ROPE_THETA = 500000.0
LN_EPS = 1e-5
N_EXPERTS = 8
LANES = 128
ADA_ROWS = 16
VMEM_LIMIT = 56 * 1024 * 1024
NEG_INF = float("-inf")

_NN = (((1,), (0,)), ((), ()))
_NT = (((1,), (1,)), ((), ()))


def _params(*sem):
    return pltpu.CompilerParams(dimension_semantics=sem, vmem_limit_bytes=VMEM_LIMIT)


def _sigmoid(x):
    return 1.0 / (1.0 + jnp.exp(-x))


def _gelu(x):
    return 0.5 * x * (1.0 + jnp.tanh(0.7978845608028654 * (x + 0.044715 * (x * x * x))))


def _layer_norm(x, g, b):
    mu = jnp.mean(x, axis=-1, keepdims=True)
    xc = x - mu
    var = jnp.mean(xc * xc, axis=-1, keepdims=True)
    return xc * lax.rsqrt(var + LN_EPS) * g + b


def _dot(a, b, dims=_NN):
    return lax.dot_general(a, b, dims, preferred_element_type=F32)


def _split_bf16(a):
    hi = a.astype(BF16)
    lo = (a - hi.astype(F32)).astype(BF16)
    return hi, lo


def _dot3(a, b, dims=_NN):
    ah, al = _split_bf16(a)
    bh, bl = _split_bf16(b)
    return _dot(ah, bh, dims) + (_dot(ah, bl, dims) + _dot(al, bh, dims))


def _ada_kernel(c_ref, w_ref, b_ref, o_ref):
    c = c_ref[...]
    a = (c * _sigmoid(c)).astype(BF16)
    o_ref[...] = _dot(a, w_ref[...].astype(BF16)) + b_ref[...]


def _ada(c_all, w_ada, b_ada):
    depth, d, n = w_ada.shape
    bn = 1024
    return pl.pallas_call(
        _ada_kernel,
        out_shape=jax.ShapeDtypeStruct((depth, ADA_ROWS, n), F32),
        grid=(depth, n // bn),
        in_specs=[
            pl.BlockSpec((ADA_ROWS, d), lambda l, j: (0, 0)),
            pl.BlockSpec((None, d, bn), lambda l, j: (l, 0, j)),
            pl.BlockSpec((None, 1, bn), lambda l, j: (l, 0, j)),
        ],
        out_specs=pl.BlockSpec((None, ADA_ROWS, bn), lambda l, j: (l, 0, j)),
        compiler_params=_params("arbitrary", "arbitrary"),
        name="ada",
    )(c_all, w_ada, b_ada.reshape(depth, 1, n))


def _mod_spec(mod3, chunk, d, decode, tiles_per_seq):
    rows = mod3.shape[1]
    if decode:
        return pl.BlockSpec((None, rows, d), lambda i, *_: (0, 0, chunk))
    return pl.BlockSpec((None, rows, d), lambda i, *_: (i // tiles_per_seq, 0, chunk))


def _rope_tables(pos):
    half = ROT_DIM // 2
    inv = ROPE_THETA ** (-jnp.arange(0, ROT_DIM, 2, dtype=F32) / ROT_DIM)
    ang = pos.astype(F32)[:, None] * inv[None, :]
    cos, sin = jnp.cos(ang), jnp.sin(ang)
    r = pos.shape[0]
    ones = jnp.ones((r, HEAD_DIM - ROT_DIM), F32)
    zeros = jnp.zeros((r, HEAD_DIM - ROT_DIM), F32)
    zhalf = jnp.zeros((r, half), F32)
    c = jnp.concatenate([cos, cos, ones], axis=1)
    s_lo = jnp.concatenate([zhalf, sin, zeros], axis=1)
    s_hi = jnp.concatenate([-sin, zhalf, zeros], axis=1)
    return c, s_lo, s_hi


PROJ_ROW_CHUNK = 128


def _proj_kernel(x_ref, sc_ref, sh_ref, w_ref, c_ref, slo_ref, shi_ref, o_ref, k_ref, v_ref, h_scr):
    j = pl.program_id(1)

    @pl.when(j == 0)
    def _():
        h_scr[...] = (x_ref[...] * (1.0 + sc_ref[...]) + sh_ref[...]).astype(BF16)

    rows, width = o_ref.shape
    chunk = min(rows, PROJ_ROW_CHUNK)

    def row_chunks(epilogue):
        for r in range(0, rows, chunk):
            rs = slice(r, r + chunk)
            epilogue(rs, _dot(h_scr[rs, :], w_ref[...]))

    def per_head(ref, rs, val):
        for h in range(N_HEADS):
            ref[rs, h, :] = val[:, h * HEAD_DIM:(h + 1) * HEAD_DIM]

    def rope(rs, acc):
        reps = width // HEAD_DIM
        c = jnp.concatenate([c_ref[rs, :]] * reps, axis=1)
        s_lo = jnp.concatenate([slo_ref[rs, :]] * reps, axis=1)
        s_hi = jnp.concatenate([shi_ref[rs, :]] * reps, axis=1)
        half = ROT_DIM // 2
        return acc * c + pltpu.roll(acc, half, 1) * s_lo + pltpu.roll(acc, width - half, 1) * s_hi

    def store(rs, val):
        o_ref[rs, :] = val

    def store_k(rs, acc):
        val = rope(rs, acc)
        o_ref[rs, :] = val
        per_head(k_ref, rs, val)

    def store_v(rs, acc):
        o_ref[rs, :] = acc
        per_head(v_ref, rs, acc)

    pl.when(j < 2)(lambda: row_chunks(lambda rs, acc: store(rs, _gelu(acc))))
    pl.when(j == 2)(lambda: row_chunks(lambda rs, acc: store(rs, rope(rs, acc))))
    pl.when(j == 3)(lambda: row_chunks(store_k))
    pl.when(j == 4)(lambda: row_chunks(store_v))
    pl.when(j > 4)(lambda: row_chunks(lambda rs, acc: store(rs, _sigmoid(acc))))


def _proj(x, mod3, w_in_b, tabs, *, decode, seq_len):
    m, d = x.shape
    n = w_in_b.shape[1]
    bn = 1024
    assert n == 9 * bn and d == 2 * bn
    bm = m if decode else min(512, seq_len)
    tps = 1 if decode else seq_len // bm
    if decode:
        tab_spec = pl.BlockSpec((bm, HEAD_DIM), lambda i, j: (0, 0))
    else:
        tab_spec = pl.BlockSpec((bm, HEAD_DIM), lambda i, j: (i % tps, 0))
    kv_shape = jax.ShapeDtypeStruct((m, N_HEADS, HEAD_DIM), F32)
    kv_spec = pl.BlockSpec((bm, N_HEADS, HEAD_DIM), lambda i, j: (i, 0, 0))
    return pl.pallas_call(
        _proj_kernel,
        out_shape=(jax.ShapeDtypeStruct((m, n), F32), kv_shape, kv_shape),
        grid=(m // bm, n // bn),
        in_specs=[
            pl.BlockSpec((bm, d), lambda i, j: (i, 0)),
            _mod_spec(mod3, 1, d, decode, tps),
            _mod_spec(mod3, 0, d, decode, tps),
            pl.BlockSpec((d, bn), lambda i, j: (0, j)),
            tab_spec, tab_spec, tab_spec,
        ],
        out_specs=(pl.BlockSpec((bm, bn), lambda i, j: (i, j)), kv_spec, kv_spec),
        scratch_shapes=[pltpu.VMEM((bm, d), BF16)],
        compiler_params=_params("arbitrary", "arbitrary"),
        name="proj_dec" if decode else "proj",
    )(x, mod3, mod3, w_in_b, *tabs)


MOBA_HEADS_PER_STEP = 2


MASKED = -1e30
LOG2_E = 1.4426950408889634


def _moba_kernel(q_ref, k_ref, v_ref, o_ref, ka_scr, qa_scr, vt_scr, km_scr, s_scr, *, nb, nbp, hp):
    qi = pl.program_id(2)
    blk = MOBA_BLOCK
    pair = 2 * blk
    t_len = nb * blk
    heads = range(hp)
    cols = [slice(h * HEAD_DIM, (h + 1) * HEAD_DIM) for h in heads]
    lane = lax.broadcasted_iota(jnp.int32, (blk, HEAD_DIM), 1)

    @pl.when(qi == 0)
    def _():
        ones_row = jnp.where(lax.broadcasted_iota(jnp.int32, (16, pair), 0) == 0, 1.0, 0.0).astype(BF16)
        for h in heads:
            for j in range(nb + 2):
                rows = slice(j * blk, (j + 1) * blk)
                if j < nb:
                    kj = k_ref[rows, cols[h]]
                    km_scr[h, j:j + 1, :] = jnp.sum(kj, axis=0, keepdims=True) * (1.0 / blk)
                    ka_scr[h, rows, 0:HEAD_DIM] = kj.astype(BF16)
                else:
                    ka_scr[h, rows, 0:HEAD_DIM] = jnp.zeros((blk, HEAD_DIM), BF16)
                ka_scr[h, rows, HEAD_DIM:] = jnp.where(lane == min(j, nb), 1.0, 0.0).astype(BF16)
            for j in range(nb // 2):
                vt_scr[h, j, 0:HEAD_DIM, :] = v_ref[j * pair:(j + 1) * pair, cols[h]].T.astype(BF16)
                vt_scr[h, j, HEAD_DIM:, :] = ones_row
            qa_scr[h, HEAD_DIM + nbp:, :] = jnp.zeros((HEAD_DIM - nbp, blk), BF16)

    n_idx = lax.broadcasted_iota(jnp.int32, (nb, blk), 0)
    for h in heads:
        q = q_ref[:, cols[h]]
        qa_scr[h, 0:HEAD_DIM, :] = (q * (HEAD_DIM ** -0.5 * LOG2_E)).T.astype(BF16)
        score = _dot3(km_scr[h], (q * (HEAD_DIM ** -0.5)).T)
        cnt = jnp.zeros((nb, blk), F32)
        for m in range(nb):
            row = score[m:m + 1, :]
            beats = jnp.where(row > score, 1.0, jnp.where(row == score, jnp.where(m < n_idx, 1.0, 0.0), 0.0))
            cnt = cnt + jnp.where(m < qi, beats, 0.0)
        past = jnp.where(cnt < MOBA_TOPK, 0.0, MASKED)
        bias = jnp.where(n_idx < qi, past, jnp.where(n_idx == qi, 0.0, MASKED))
        bias = jnp.concatenate([bias, jnp.full((nbp - nb, blk), MASKED, F32)], axis=0)
        qa_scr[h, HEAD_DIM:HEAD_DIM + nbp, :] = bias.astype(BF16)

    def pair_scores(h, t):
        return _dot(ka_scr[h, pl.ds(pl.multiple_of(t * pair, pair), pair), :], qa_scr[h])

    def online_step(state, h, slot, t):
        m_run, acc = state
        s = s_scr[slot, h]
        m_new = jnp.maximum(m_run, jnp.max(s, axis=0, keepdims=True))
        p = jnp.exp2(s - m_new).astype(BF16)
        return [m_new, jnp.exp2(m_run - m_new) * acc + _dot(vt_scr[h, t], p)]

    t_own = lax.shift_right_logical(qi, 1)
    last = jnp.maximum(t_own - 1, 0)
    rel = lax.broadcasted_iota(jnp.int32, (pair, blk), 0) - (qi - 2 * t_own) * blk
    qry_i = lax.broadcasted_iota(jnp.int32, (pair, blk), 1)
    for h in heads:
        s = pair_scores(h, t_own)
        s_scr[0, h] = jnp.where(rel > qry_i, jnp.where(rel < blk, MASKED, s), s)

    def body(u, carry):
        t1 = 2 * u
        k1 = jnp.where(t1 < t_own, t1, nb // 2)
        v0 = jnp.where(u == 0, t_own, t1 - 1)
        v1 = jnp.minimum(t1, last)
        k2 = jnp.minimum(t1 + 1, last)
        state = [list(carry[2 * h:2 * h + 2]) for h in heads]
        for h in heads:
            s_scr[1, h] = pair_scores(h, k1)
        for h in heads:
            state[h] = online_step(state[h], h, 0, v0)
        for h in heads:
            s_scr[0, h] = pair_scores(h, k2)
        for h in heads:
            state[h] = online_step(state[h], h, 1, v1)
        return tuple(x for st in state for x in st)

    init = [jnp.full((1, blk), MASKED, F32), jnp.zeros((HEAD_DIM + 16, blk), F32)] * hp
    fin = lax.fori_loop(0, lax.shift_right_logical(t_own + 2, 1), body, tuple(init))
    for h in heads:
        acc = fin[2 * h + 1]
        o_ref[:, cols[h]] = (acc[0:HEAD_DIM, :] / acc[HEAD_DIM:HEAD_DIM + 1, :]).T


def _moba_prefill(proj3):
    b, t, _ = proj3.shape
    blk = MOBA_BLOCK
    nb = t // blk
    hp = MOBA_HEADS_PER_STEP
    assert t % blk == 0 and nb % 8 == 0 and N_HEADS % hp == 0
    w = hp * HEAD_DIM
    q_col, k_col, v_col = (c * HEAD_DIM // w for c in (16, 24, 32))
    nbp = -(-(nb + 1) // 16) * 16
    assert nbp <= HEAD_DIM
    return pl.pallas_call(
        functools.partial(_moba_kernel, nb=nb, nbp=nbp, hp=hp),
        out_shape=jax.ShapeDtypeStruct((b, t, N_HEADS * HEAD_DIM), F32),
        grid=(b, N_HEADS // hp, nb),
        in_specs=[
            pl.BlockSpec((None, blk, w), lambda bi, h, qi: (bi, qi, q_col + h)),
            pl.BlockSpec((None, t, w), lambda bi, h, qi: (bi, 0, k_col + h)),
            pl.BlockSpec((None, t, w), lambda bi, h, qi: (bi, 0, v_col + h)),
        ],
        out_specs=pl.BlockSpec((None, blk, w), lambda bi, h, qi: (bi, qi, h)),
        scratch_shapes=[
            pltpu.VMEM((hp, t + 2 * blk, 2 * HEAD_DIM), BF16),
            pltpu.VMEM((hp, 2 * HEAD_DIM, blk), BF16),
            pltpu.VMEM((hp, nb // 2, HEAD_DIM + 16, 2 * blk), BF16),
            pltpu.VMEM((hp, nb, HEAD_DIM), F32),
            pltpu.VMEM((2, hp, 2 * blk, blk), F32),
        ],
        compiler_params=_params("arbitrary", "arbitrary", "arbitrary"),
        name="moba",
    )(proj3, proj3, proj3)


def _paged_kmean_kernel(pt_ref, *refs, ppb):
    del pt_ref
    pages, o_ref = refs[:ppb], refs[ppb]
    for b in range(ppb // 2):
        tot = jnp.sum(pages[2 * b][...], axis=0) + jnp.sum(pages[2 * b + 1][...], axis=0)
        o_ref[b] = tot * (1.0 / MOBA_BLOCK)


def _paged_kmean(cache_k, page_table, layer):
    _, _, page, nh, dh = cache_k.shape
    s, n_pages = page_table.shape
    assert 2 * page == MOBA_BLOCK
    ppb = 16 if n_pages % 16 == 0 else 2
    page_specs = [
        pl.BlockSpec((None, None, page, nh, dh),
                     lambda si, c, pt, i=i: (layer, pt[si, c * ppb + i], 0, 0, 0))
        for i in range(ppb)
    ]
    return pl.pallas_call(
        functools.partial(_paged_kmean_kernel, ppb=ppb),
        out_shape=jax.ShapeDtypeStruct((s, n_pages // 2, nh, dh), F32),
        grid_spec=pltpu.PrefetchScalarGridSpec(
            num_scalar_prefetch=1,
            grid=(s, n_pages // ppb),
            in_specs=page_specs,
            out_specs=pl.BlockSpec((None, ppb // 2, nh, dh), lambda si, c, pt: (si, c, 0, 0)),
        ),
        compiler_params=_params("arbitrary", "arbitrary"),
        name="paged_kmean",
    )(page_table, *([cache_k] * ppb))


def _dec_select_kernel(km_ref, q_ref, o_ref, *, n_blocks):
    si = pl.program_id(0)
    q = q_ref[pl.ds(si, 1), :] * (HEAD_DIM ** -0.5)
    lane = lax.broadcasted_iota(jnp.int32, (8, LANES), 1).astype(F32)
    for h in range(N_HEADS):
        qh = jnp.broadcast_to(q[:, h * HEAD_DIM:(h + 1) * HEAD_DIM], (8, HEAD_DIM))
        sc = _dot3(qh, km_ref[h], _NT)
        sc = jnp.where(lane < n_blocks, sc, NEG_INF)
        row = jnp.zeros((8, LANES), F32)
        for r in range(MOBA_TOPK):
            best = jnp.max(sc, axis=1, keepdims=True)
            idx = jnp.min(jnp.where(sc == best, lane, 1e9), axis=1, keepdims=True)
            row = jnp.where(lane == r, idx, row)
            sc = jnp.where(lane == idx, NEG_INF, sc)
        o_ref[pl.ds(h, 1), :] = row[0:1, :].astype(jnp.int32)


def _dec_select(kmean, proj_s):
    s, n_blocks, nh, dh = kmean.shape
    assert MOBA_TOPK <= n_blocks <= LANES
    km = jnp.transpose(kmean, (0, 2, 1, 3))
    km = jnp.pad(km, ((0, 0), (0, 0), (0, LANES - n_blocks), (0, 0)))
    return pl.pallas_call(
        functools.partial(_dec_select_kernel, n_blocks=n_blocks),
        out_shape=jax.ShapeDtypeStruct((s, nh, LANES), jnp.int32),
        grid=(s,),
        in_specs=[
            pl.BlockSpec((None, nh, LANES, dh), lambda si: (si, 0, 0, 0)),
            pl.BlockSpec((s, nh * dh), lambda si: (0, 2)),
        ],
        out_specs=pl.BlockSpec((None, nh, LANES), lambda si: (si, 0, 0)),
        compiler_params=_params("arbitrary"),
        name="dec_select",
    )(km, proj_s)


def _dec_attn_kernel(sel_ref, pt_ref, ck_hbm, cv_hbm, q_ref, kn_ref, vn_ref, o_ref,
                     k_scr, v_scr, k_sem, v_sem, *, layer):
    si = pl.program_id(0)
    n_sel = 2 * MOBA_TOPK

    def slab_copies(h):
        for i in range(n_sel):
            j, r = divmod(i, 2)
            pid = pt_ref[si, 2 * sel_ref[(si * N_HEADS + h) * MOBA_TOPK + j] + r]
            yield pltpu.make_async_copy(ck_hbm.at[layer, pid, :, h, :], k_scr.at[h * n_sel + i], k_sem)
            yield pltpu.make_async_copy(cv_hbm.at[layer, pid, :, h, :], v_scr.at[h * n_sel + i], v_sem)

    for h in range(N_HEADS):
        for cp in slab_copies(h):
            cp.start()
    n_slabs = k_scr.shape[0]
    pltpu.make_async_copy(ck_hbm.at[layer, pl.ds(0, n_slabs), :, 0, :], k_scr, k_sem).wait()
    pltpu.make_async_copy(cv_hbm.at[layer, pl.ds(0, n_slabs), :, 0, :], v_scr, v_sem).wait()

    for h in range(N_HEADS):
        q = q_ref[h:h + 1, :] * (HEAD_DIM ** -0.5)
        s_own = jnp.sum(q * kn_ref[h:h + 1, :], axis=-1, keepdims=True)
        s_sel = [jnp.sum(k_scr[h * n_sel + i] * q, axis=-1, keepdims=True) for i in range(n_sel)]
        m = s_own
        for s_i in s_sel:
            m = jnp.maximum(m, jnp.max(s_i, axis=0, keepdims=True))
        p_own = jnp.exp(s_own - m)
        denom = p_own
        out = p_own * vn_ref[h:h + 1, :]
        for i in range(n_sel):
            p = jnp.exp(s_sel[i] - m)
            denom = denom + jnp.sum(p, axis=0, keepdims=True)
            out = out + jnp.sum(p * v_scr[h * n_sel + i], axis=0, keepdims=True)
        o_ref[h:h + 1, :] = out / denom


def _dec_attn(cache_k, cache_v, page_table, sel, q3, k3, v3, layer):
    _, n_phys, page, nh, dh = cache_k.shape
    s = page_table.shape[0]
    sel_flat = sel[:, :, :MOBA_TOPK].reshape(-1)
    n_slabs = nh * 2 * MOBA_TOPK
    assert nh == N_HEADS and n_phys >= n_slabs
    row_spec = pl.BlockSpec((None, nh, dh), lambda si, sel_r, pt_r: (si, 0, 0))
    return pl.pallas_call(
        functools.partial(_dec_attn_kernel, layer=layer),
        out_shape=jax.ShapeDtypeStruct((s, nh, dh), F32),
        grid_spec=pltpu.PrefetchScalarGridSpec(
            num_scalar_prefetch=2,
            grid=(s,),
            in_specs=[pl.BlockSpec(memory_space=pl.ANY)] * 2 + [row_spec] * 3,
            out_specs=row_spec,
            scratch_shapes=[pltpu.VMEM((n_slabs, page, dh), F32), pltpu.VMEM((n_slabs, page, dh), F32),
                            pltpu.SemaphoreType.DMA, pltpu.SemaphoreType.DMA],
        ),
        compiler_params=_params("arbitrary"),
        name="dec_attn",
    )(sel_flat, page_table, cache_k, cache_v, q3, k3, v3)


def _mix_kernel(u_ref, va_ref, attn_ref, ga0_ref, ga1_ref, gb0_ref, gb1_ref, x_ref, g1_ref,
                ws_ref, bs_ref, lvg_ref, lvb_ref, wba_ref, wbb_ref, wo_ref, l1g_ref, l1b_ref,
                *out_refs, decode, alpha):
    vn = _layer_norm(va_ref[...], lvg_ref[...], lvb_ref[...])
    if decode:
        out_refs[1][...] = vn
        mixed = vn * ws_ref[...] + bs_ref[...]
    else:
        rows = vn.shape[0]
        vnb = vn.astype(BF16)
        r_i = lax.broadcasted_iota(jnp.int32, (CHUNK, CHUNK), 0)
        c_i = lax.broadcasted_iota(jnp.int32, (CHUNK, CHUNK), 1)
        w_tril = [jnp.where(c_i <= r_i, ws_ref[g], 0.0).astype(BF16) for g in range(SGU_GROUPS)]
        gd = vn.shape[1] // SGU_GROUPS
        chunks = []
        for c in range(rows // CHUNK):
            cols = []
            for g in range(SGU_GROUPS):
                v_cg = vnb[c * CHUNK:(c + 1) * CHUNK, g * gd:(g + 1) * gd]
                cols.append(_dot(w_tril[g], v_cg) + bs_ref[:, g:g + 1])
            chunks.append(jnp.concatenate(cols, axis=1))
        mixed = jnp.concatenate(chunks, axis=0)
    a_pre = (u_ref[...] * mixed).astype(BF16)
    a_out = _dot(a_pre, wba_ref[...])
    b_out = _dot(attn_ref[...].astype(BF16), wbb_ref[...])
    g_a = jnp.concatenate([ga0_ref[...], ga1_ref[...]], axis=1)
    g_b = jnp.concatenate([gb0_ref[...], gb1_ref[...]], axis=1)
    merged = (g_a * a_out + g_b * b_out).astype(BF16)
    y = _dot(merged, wo_ref[...])
    out_refs[0][...] = _layer_norm(alpha * x_ref[...] + (1.0 + g1_ref[...]) * y, l1g_ref[...], l1b_ref[...])


def _mix(proj, attn, x, mod3, w_s, b_s, ln_v_g, ln_v_b, w_ba_b, w_bb_b, w_o_b, ln1_g, ln1_b,
         *, decode, seq_len, alpha):
    m, d = x.shape
    sw = w_ba_b.shape[0]
    bm = m if decode else min(256, seq_len)
    tps = 1 if decode else seq_len // bm
    const = lambda *_: (0, 0)
    single = pl.Buffered(1)

    def col(cidx):
        return pl.BlockSpec((bm, sw), lambda i: (i, cidx))

    if decode:
        gd = sw // SGU_GROUPS
        ws_arg = jnp.repeat(w_s[:, 0, 0], gd)[None, :]
        bs_arg = jnp.repeat(b_s[:, 0], gd)[None, :]
        ws_spec = pl.BlockSpec((1, sw), const)
        bs_spec = pl.BlockSpec((1, sw), const)
        out_shape = (jax.ShapeDtypeStruct((m, d), F32), jax.ShapeDtypeStruct((m, sw), F32))
        out_specs = (pl.BlockSpec((bm, d), lambda i: (i, 0)), pl.BlockSpec((bm, sw), lambda i: (i, 0)))
    else:
        ws_arg, bs_arg = w_s, b_s.T
        ws_spec = pl.BlockSpec(w_s.shape, lambda i: (0, 0, 0))
        bs_spec = pl.BlockSpec((CHUNK, SGU_GROUPS), const)
        out_shape = (jax.ShapeDtypeStruct((m, d), F32),)
        out_specs = (pl.BlockSpec((bm, d), lambda i: (i, 0)),)

    return pl.pallas_call(
        functools.partial(_mix_kernel, decode=decode, alpha=alpha),
        out_shape=out_shape,
        grid=(m // bm,),
        in_specs=[
            col(0), col(1),
            pl.BlockSpec((bm, sw), lambda i: (i, 0)),
            col(5), col(6), col(7), col(8),
            pl.BlockSpec((bm, d), lambda i: (i, 0)),
            _mod_spec(mod3, 2, d, decode, tps),
            ws_spec, bs_spec,
            pl.BlockSpec((1, sw), const), pl.BlockSpec((1, sw), const),
            pl.BlockSpec((sw, d), const, pipeline_mode=single),
            pl.BlockSpec((sw, d), const, pipeline_mode=single),
            pl.BlockSpec((d, d), const, pipeline_mode=single),
            pl.BlockSpec((1, d), const), pl.BlockSpec((1, d), const),
        ],
        out_specs=out_specs,
        compiler_params=_params("arbitrary"),
        name="mix_dec" if decode else "mix",
    )(proj, proj, attn, proj, proj, proj, proj, x, mod3, ws_arg, bs_arg,
      ln_v_g[None, :], ln_v_b[None, :], w_ba_b, w_bb_b, w_o_b, ln1_g[None, :], ln1_b[None, :])


def _ffn_kernel(x_ref, sc_ref, sh_ref, g_ref, wg_ref, wu_ref, wd_ref, lg_ref, lb_ref, o_ref,
                h_scr, acc_scr, *, alpha):
    f = pl.program_id(1)

    @pl.when(f == 0)
    def _():
        h_scr[...] = (x_ref[...] * (1.0 + sc_ref[...]) + sh_ref[...]).astype(BF16)
        acc_scr[...] = jnp.zeros_like(acc_scr)

    h = h_scr[...]
    a = _dot(h, wg_ref[...])
    act = (a * _sigmoid(a) * _dot(h, wu_ref[...])).astype(BF16)
    acc_scr[...] += _dot(act, wd_ref[...])

    @pl.when(f == pl.num_programs(1) - 1)
    def _():
        y = alpha * x_ref[...] + (1.0 + g_ref[...]) * acc_scr[...]
        o_ref[...] = _layer_norm(y, lg_ref[...], lb_ref[...])


def _ffn(x, mod3, wg_b, wu_b, wd_b, ln_g, ln_b, *, decode, seq_len, alpha):
    m, d = x.shape
    ff = wg_b.shape[1]
    bf = 512 if ff % 512 == 0 else ff
    bm = m if decode else min(512, seq_len)
    tps = 1 if decode else seq_len // bm
    const = lambda *_: (0, 0)
    return pl.pallas_call(
        functools.partial(_ffn_kernel, alpha=alpha),
        out_shape=jax.ShapeDtypeStruct((m, d), F32),
        grid=(m // bm, ff // bf),
        in_specs=[
            pl.BlockSpec((bm, d), lambda i, f: (i, 0)),
            _mod_spec(mod3, 4, d, decode, tps),
            _mod_spec(mod3, 3, d, decode, tps),
            _mod_spec(mod3, 5, d, decode, tps),
            pl.BlockSpec((d, bf), lambda i, f: (0, f)),
            pl.BlockSpec((d, bf), lambda i, f: (0, f)),
            pl.BlockSpec((bf, d), lambda i, f: (f, 0)),
            pl.BlockSpec((1, d), const), pl.BlockSpec((1, d), const),
        ],
        out_specs=pl.BlockSpec((bm, d), lambda i, f: (i, 0)),
        scratch_shapes=[pltpu.VMEM((bm, d), BF16), pltpu.VMEM((bm, d), F32)],
        compiler_params=_params("arbitrary", "arbitrary"),
        name="ffn_dec" if decode else "ffn",
    )(x, mod3, mod3, mod3, wg_b, wu_b, wd_b, ln_g[None, :], ln_b[None, :])


def _moe_route_kernel(x_ref, sc_ref, sh_ref, r_ref, cnt0_ref, h_ref, route_ref, cnt_ref, cnt_scr):
    i = pl.program_id(0)
    rows = x_ref.shape[0]

    @pl.when(i == 0)
    def _():
        cnt_scr[...] = cnt0_ref[...]

    h = x_ref[...] * (1.0 + sc_ref[...]) + sh_ref[...]
    h_ref[...] = h
    lane = lax.broadcasted_iota(jnp.int32, (rows, LANES), 1)
    lane_f = lane.astype(F32)
    logits = jnp.where(lane < N_EXPERTS, _dot3(h, r_ref[...]), NEG_INF)
    v1 = jnp.max(logits, axis=1, keepdims=True)
    i1 = jnp.min(jnp.where(logits == v1, lane_f, 1e9), axis=1, keepdims=True)
    rest = jnp.where(lane_f == i1, NEG_INF, logits)
    v2 = jnp.max(rest, axis=1, keepdims=True)
    i2 = jnp.min(jnp.where(rest == v2, lane_f, 1e9), axis=1, keepdims=True)
    ex = jnp.exp(v2 - v1)
    w1 = 1.0 / (1.0 + ex)
    w2 = ex / (1.0 + ex)
    picks = jnp.where(lane_f == i1, 1.0, jnp.where(lane_f == i2, 1.0, 0.0))
    if rows >= LANES:
        r_i = lax.broadcasted_iota(jnp.int32, (rows, rows), 0)
        c_i = lax.broadcasted_iota(jnp.int32, (rows, rows), 1)
        before = _dot(jnp.where(c_i < r_i, 1.0, 0.0).astype(BF16), picks.astype(BF16))
    else:
        r_i = lax.broadcasted_iota(jnp.int32, (rows, LANES), 0)
        before = jnp.zeros((rows, LANES), F32)
        for t in range(rows - 1):
            before = before + jnp.where(r_i > t, picks[t:t + 1, :], 0.0)
    before = before + cnt_scr[0:1, :]
    rank1 = jnp.sum(jnp.where(lane_f == i1, before, 0.0), axis=1, keepdims=True)
    rank2 = jnp.sum(jnp.where(lane_f == i2, before, 0.0), axis=1, keepdims=True)
    rec = jnp.zeros((rows, LANES), F32)
    for k, val in enumerate((i1, i2, w1, w2, rank1, rank2)):
        rec = jnp.where(lane == k, val, rec)
    route_ref[...] = rec
    cnt_scr[...] = cnt_scr[...] + jnp.sum(picks, axis=0, keepdims=True)
    cnt_ref[...] = cnt_scr[...]


def _moe_route(x, mod3, router, cnt0, *, decode, seq_len):
    m, d = x.shape
    bm = m if decode else min(512, seq_len)
    tps = 1 if decode else seq_len // bm
    router_p = jnp.pad(router, ((0, 0), (0, LANES - router.shape[1])))
    return pl.pallas_call(
        _moe_route_kernel,
        out_shape=(jax.ShapeDtypeStruct((m, d), F32), jax.ShapeDtypeStruct((m, LANES), F32),
                   jax.ShapeDtypeStruct((8, LANES), F32)),
        grid=(m // bm,),
        in_specs=[
            pl.BlockSpec((bm, d), lambda i: (i, 0)),
            _mod_spec(mod3, 4, d, decode, tps),
            _mod_spec(mod3, 3, d, decode, tps),
            pl.BlockSpec((d, LANES), lambda i: (0, 0)),
            pl.BlockSpec((8, LANES), lambda i: (0, 0)),
        ],
        out_specs=(pl.BlockSpec((bm, d), lambda i: (i, 0)), pl.BlockSpec((bm, LANES), lambda i: (i, 0)),
                   pl.BlockSpec((8, LANES), lambda i: (0, 0))),
        scratch_shapes=[pltpu.VMEM((8, LANES), F32)],
        compiler_params=_params("arbitrary"),
        name="moe_route_dec" if decode else "moe_route",
    )(x, mod3, mod3, router_p, cnt0)


def _row_gather_start(src_hbm, idx_ref, base, dst, sem):
    def issue(r, carry):
        pltpu.make_async_copy(src_hbm.at[pl.ds(idx_ref[base + r], 1), :], dst.at[pl.ds(r, 1), :], sem).start()
        return carry

    lax.fori_loop(0, dst.shape[0], issue, 0)


def _row_gather_wait(src_hbm, dst, sem):
    pltpu.make_async_copy(src_hbm.at[pl.ds(0, dst.shape[0]), :], dst, sem).wait()


MOE_HALF_ROWS = 512


def _moe_group_kernel(te_ref, nh_ref, tok_ref, h_hbm, wg_ref, wu_ref, wd_ref, y_ref, xg_scr, hb_scr, sem):
    del te_ref
    i = pl.program_id(0)
    f = pl.program_id(1)
    half = xg_scr.shape[0]
    n_halves = nh_ref[i]

    @pl.when(f == 0)
    def _():
        y_ref[...] = jnp.zeros_like(y_ref)

    for hf in range(2):
        @pl.when((f == 0) & (n_halves > hf))
        def _(hf=hf):
            _row_gather_start(h_hbm, tok_ref, (2 * i + hf) * half, xg_scr, sem)
            _row_gather_wait(h_hbm, xg_scr, sem)
            hb_scr[hf] = xg_scr[...].astype(BF16)

    def compute(halves):
        wg, wu, wd = (w[...].astype(BF16) for w in (wg_ref, wu_ref, wd_ref))
        for hf in range(halves):
            h = hb_scr[hf]
            a = _dot(h, wg)
            act = (a * _sigmoid(a) * _dot(h, wu)).astype(BF16)
            y_ref[hf * half:(hf + 1) * half, :] += _dot(act, wd)

    pl.when(n_halves == 1)(lambda: compute(1))
    pl.when(n_halves == 2)(lambda: compute(2))


def _moe_group(h_all, tile_expert, tile_halves, tok_of_slot, wg, wu, wd):
    d = h_all.shape[1]
    ff = wg.shape[2]
    bf = 256 if ff % 256 == 0 else ff
    nf = ff // bf
    half = MOE_HALF_ROWS
    n_tiles = tile_expert.shape[0]

    def f_idx(i, f, nh):
        return jnp.where(nh[i] > 0, f, nf - 1)

    return pl.pallas_call(
        _moe_group_kernel,
        out_shape=jax.ShapeDtypeStruct((n_tiles * 2 * half, d), F32),
        grid_spec=pltpu.PrefetchScalarGridSpec(
            num_scalar_prefetch=3,
            grid=(n_tiles, nf),
            in_specs=[
                pl.BlockSpec(memory_space=pl.ANY),
                pl.BlockSpec((None, d, bf), lambda i, f, te, nh, tok: (te[i], 0, f_idx(i, f, nh))),
                pl.BlockSpec((None, d, bf), lambda i, f, te, nh, tok: (te[i], 0, f_idx(i, f, nh))),
                pl.BlockSpec((None, bf, d), lambda i, f, te, nh, tok: (te[i], f_idx(i, f, nh), 0)),
            ],
            out_specs=pl.BlockSpec((2 * half, d), lambda i, f, te, nh, tok: (i, 0)),
            scratch_shapes=[pltpu.VMEM((half, d), F32), pltpu.VMEM((2, half, d), BF16),
                            pltpu.SemaphoreType.DMA],
        ),
        compiler_params=_params("arbitrary", "arbitrary"),
        name="moe_group",
    )(tile_expert, tile_halves, tok_of_slot, h_all, wg, wu, wd)


def _moe_combine_kernel(s1_ref, s2_ref, y_hbm, route_ref, x_ref, g_ref, lg_ref, lb_ref, o_ref,
                        y1_scr, y2_scr, sem1, sem2, *, alpha):
    i = pl.program_id(0)
    rows = x_ref.shape[0]
    _row_gather_start(y_hbm, s1_ref, i * rows, y1_scr, sem1)
    _row_gather_start(y_hbm, s2_ref, i * rows, y2_scr, sem2)
    _row_gather_wait(y_hbm, y1_scr, sem1)
    _row_gather_wait(y_hbm, y2_scr, sem2)
    route = route_ref[...]
    f = route[:, 2:3] * y1_scr[...] + route[:, 3:4] * y2_scr[...]
    y = alpha * x_ref[...] + (1.0 + g_ref[...]) * f
    o_ref[...] = _layer_norm(y, lg_ref[...], lb_ref[...])


def _moe_combine(y_sorted, slot1, slot2, route, x, mod3, ln_g, ln_b, *, decode, seq_len, alpha):
    m, d = x.shape
    bm = m if decode else min(256, seq_len)
    tps = 1 if decode else seq_len // bm
    rows_mod = mod3.shape[1]
    if decode:
        g_spec = pl.BlockSpec((None, rows_mod, d), lambda i, s1, s2: (0, 0, 5))
    else:
        g_spec = pl.BlockSpec((None, rows_mod, d), lambda i, s1, s2: (i // tps, 0, 5))
    const = lambda i, s1, s2: (0, 0)
    return pl.pallas_call(
        functools.partial(_moe_combine_kernel, alpha=alpha),
        out_shape=jax.ShapeDtypeStruct((m, d), F32),
        grid_spec=pltpu.PrefetchScalarGridSpec(
            num_scalar_prefetch=2,
            grid=(m // bm,),
            in_specs=[
                pl.BlockSpec(memory_space=pl.ANY),
                pl.BlockSpec((bm, LANES), lambda i, s1, s2: (i, 0)),
                pl.BlockSpec((bm, d), lambda i, s1, s2: (i, 0)),
                g_spec,
                pl.BlockSpec((1, d), const), pl.BlockSpec((1, d), const),
            ],
            out_specs=pl.BlockSpec((bm, d), lambda i, s1, s2: (i, 0)),
            scratch_shapes=[pltpu.VMEM((bm, d), F32), pltpu.VMEM((bm, d), F32),
                            pltpu.SemaphoreType.DMA, pltpu.SemaphoreType.DMA],
        ),
        compiler_params=_params("arbitrary"),
        name="moe_combine_dec" if decode else "moe_combine",
    )(slot1, slot2, y_sorted, route, x, mod3, ln_g[None, :], ln_b[None, :])


def _moe(xp, xs, mod_p, mod_s, router, wg, wu, wd, ln_g, ln_b, *, seq_len, alpha):
    mp, ms = xp.shape[0], xs.shape[0]
    m = mp + ms
    ne = wg.shape[0]
    half = MOE_HALF_ROWS
    n_tiles = -(-m // half) + ne
    h_p, route_p, cnt_p = _moe_route(xp, mod_p, router, jnp.zeros((8, LANES), F32), decode=False, seq_len=seq_len)
    h_s, route_s, cnt = _moe_route(xs, mod_s, router, cnt_p, decode=True, seq_len=1)
    h_all = jnp.concatenate([h_p, h_s], axis=0)
    route = jnp.concatenate([route_p, route_s], axis=0)

    counts = cnt[0, :ne].astype(jnp.int32)
    n_half = (counts + half - 1) // half
    cap = (n_half + 1) // 2 * (2 * half)
    ends = jnp.cumsum(cap)
    starts = ends - cap
    used_end = starts + n_half * half
    e1, e2 = route[:, 0].astype(jnp.int32), route[:, 1].astype(jnp.int32)
    slot1 = starts[e1] + route[:, 4].astype(jnp.int32)
    slot2 = starts[e2] + route[:, 5].astype(jnp.int32)
    tok = jnp.arange(m, dtype=jnp.int32)
    tok_of_slot = jnp.zeros((n_tiles * 2 * half,), jnp.int32).at[slot1].set(tok).at[slot2].set(tok)
    tile_lo = jnp.arange(n_tiles, dtype=jnp.int32) * (2 * half)
    last_lo = jnp.maximum(ends[-1] - 2 * half, 0)
    tile_expert = jnp.sum((jnp.minimum(tile_lo, last_lo)[:, None] >= ends[None, :]).astype(jnp.int32), axis=1)
    tile_expert = jnp.minimum(tile_expert, ne - 1)
    tile_halves = jnp.where(tile_lo >= ends[-1], 0, jnp.where(tile_lo + half < used_end[tile_expert], 2, 1))

    y_sorted = _moe_group(h_all, tile_expert, tile_halves.astype(jnp.int32), tok_of_slot, wg, wu, wd)
    out_p = _moe_combine(y_sorted, slot1[:mp], slot2[:mp], route_p, xp, mod_p, ln_g, ln_b,
                         decode=False, seq_len=seq_len, alpha=alpha)
    out_s = _moe_combine(y_sorted, slot1[mp:], slot2[mp:], route_s, xs, mod_s, ln_g, ln_b,
                         decode=True, seq_len=1, alpha=alpha)
    return out_p, out_s


def kernel(x_prompt, x_sample, cache_k, cache_v, page_table, c_prompt, c_sample,
           w_in, ln_v_g, ln_v_b, w_s, b_s, w_ba, w_bb, w_o, w_ada, b_ada,
           ln1_g, ln1_b, ln2_g, ln2_b, ffn_w_gate, ffn_w_up, ffn_w_down,
           moe_router, moe_w_gate, moe_w_up, moe_w_down):
    depth = w_in.shape[0]
    b, t, d = x_prompt.shape
    s = x_sample.shape[0]
    assert x_sample.shape[1] == 1 and s % 8 == 0 and b + s <= ADA_ROWS
    n_pages, page = page_table.shape[1], cache_k.shape[2]
    past_len = n_pages * page
    alpha = (2 * depth) ** 0.25
    sw = w_ba.shape[1]

    c_all = jnp.concatenate([c_prompt, c_sample, jnp.zeros((ADA_ROWS - b - s, d), F32)], axis=0)
    mod_all = _ada(c_all, w_ada, b_ada)

    tabs_p = _rope_tables(jnp.arange(t))
    tabs_s = _rope_tables(jnp.full((s,), past_len, jnp.int32))

    xp = x_prompt.reshape(b * t, d)
    xs = x_sample.reshape(s, d)
    k_p, v_p, k_s, v_s, vn_s = [], [], [], [], []
    aw = N_HEADS * HEAD_DIM
    q_lo = 2 * sw
    for l in range(depth):
        mod_p = mod_all[l, :b][:, None, :]
        mod_s = mod_all[l, b:b + s][None]
        w_in_b = w_in[l].astype(BF16)
        mix_w = (w_s[l], b_s[l], ln_v_g[l], ln_v_b[l], w_ba[l].astype(BF16), w_bb[l].astype(BF16),
                 w_o[l].astype(BF16), ln1_g[l], ln1_b[l])
        proj_p, k_new, v_new = _proj(xp, mod_p, w_in_b, tabs_p, decode=False, seq_len=t)
        attn_p = _moba_prefill(proj_p.reshape(b, t, -1)).reshape(b * t, -1)
        (xp1,) = _mix(proj_p, attn_p, xp, mod_p, *mix_w, decode=False, seq_len=t, alpha=alpha)
        k_p.append(k_new.reshape(b, t, N_HEADS, HEAD_DIM))
        v_p.append(v_new.reshape(b, t, N_HEADS, HEAD_DIM))

        proj_s, k3, v3 = _proj(xs, mod_s, w_in_b, tabs_s, decode=True, seq_len=1)
        kmean = _paged_kmean(cache_k, page_table, l)
        sel = _dec_select(kmean, proj_s)
        q3 = proj_s[:, q_lo:q_lo + aw].reshape(s, N_HEADS, HEAD_DIM)
        attn_s = _dec_attn(cache_k, cache_v, page_table, sel, q3, k3, v3, l).reshape(s, aw)
        xs1, vn = _mix(proj_s, attn_s, xs, mod_s, *mix_w, decode=True, seq_len=1, alpha=alpha)

        if l % 2 == 0:
            ffn_w = (ffn_w_gate[l // 2].astype(BF16), ffn_w_up[l // 2].astype(BF16),
                     ffn_w_down[l // 2].astype(BF16), ln2_g[l], ln2_b[l])
            xp = _ffn(xp1, mod_p, *ffn_w, decode=False, seq_len=t, alpha=alpha)
            xs = _ffn(xs1, mod_s, *ffn_w, decode=True, seq_len=1, alpha=alpha)
        else:
            xp, xs = _moe(xp1, xs1, mod_p, mod_s, moe_router[l // 2], moe_w_gate[l // 2], moe_w_up[l // 2],
                          moe_w_down[l // 2], ln2_g[l], ln2_b[l], seq_len=t, alpha=alpha)
        k_s.append(k3.reshape(s, 1, N_HEADS, HEAD_DIM))
        v_s.append(v3.reshape(s, 1, N_HEADS, HEAD_DIM))
        vn_s.append(vn.reshape(s, 1, sw))

    return (xp.reshape(b, t, d), xs.reshape(s, 1, d), jnp.stack(k_p), jnp.stack(v_p),
            jnp.stack(k_s), jnp.stack(v_s), jnp.stack(vn_s))
```

```python
import functools

import jax
import jax.numpy as jnp
from jax import lax
from jax.experimental import pallas as pl
from jax.experimental.pallas import tpu as pltpu

F32 = jnp.float32
BF16 = jnp.bfloat16

CHUNK = 128
SGU_GROUPS = 8
N_HEADS = 8
HEAD_DIM = 128
ROT_DIM = HEAD_DIM // 4
MOBA_BLOCK = 256
MOBA_TOPK = 3
ROPE_THETA = 500000.0
LN_EPS = 1e-5
N_EXPERTS = 8
LANES = 128
ADA_ROWS = 16
VMEM_LIMIT = 56 * 1024 * 1024
NEG_INF = float("-inf")

_NN = (((1,), (0,)), ((), ()))
_NT = (((1,), (1,)), ((), ()))


def _params(*sem):
    return pltpu.CompilerParams(dimension_semantics=sem, vmem_limit_bytes=VMEM_LIMIT)


def _sigmoid(x):
    return 1.0 / (1.0 + jnp.exp(-x))


def _gelu(x):
    return 0.5 * x * (1.0 + jnp.tanh(0.7978845608028654 * (x + 0.044715 * (x * x * x))))


def _layer_norm(x, g, b):
    mu = jnp.mean(x, axis=-1, keepdims=True)
    xc = x - mu
    var = jnp.mean(xc * xc, axis=-1, keepdims=True)
    return xc * lax.rsqrt(var + LN_EPS) * g + b


def _dot(a, b, dims=_NN):
    return lax.dot_general(a, b, dims, preferred_element_type=F32)


def _split_bf16(a):
    hi = a.astype(BF16)
    lo = (a - hi.astype(F32)).astype(BF16)
    return hi, lo


def _dot3(a, b, dims=_NN):
    ah, al = _split_bf16(a)
    bh, bl = _split_bf16(b)
    return _dot(ah, bh, dims) + (_dot(ah, bl, dims) + _dot(al, bh, dims))


def _ada_kernel(c_ref, w_ref, b_ref, o_ref):
    c = c_ref[...]
    a = (c * _sigmoid(c)).astype(BF16)
    o_ref[...] = _dot(a, w_ref[...].astype(BF16)) + b_ref[...]


def _ada(c_all, w_ada, b_ada):
    depth, d, n = w_ada.shape
    bn = 1024
    return pl.pallas_call(
        _ada_kernel,
        out_shape=jax.ShapeDtypeStruct((depth, ADA_ROWS, n), F32),
        grid=(depth, n // bn),
        in_specs=[
            pl.BlockSpec((ADA_ROWS, d), lambda l, j: (0, 0)),
            pl.BlockSpec((None, d, bn), lambda l, j: (l, 0, j)),
            pl.BlockSpec((None, 1, bn), lambda l, j: (l, 0, j)),
        ],
        out_specs=pl.BlockSpec((None, ADA_ROWS, bn), lambda l, j: (l, 0, j)),
        compiler_params=_params("arbitrary", "arbitrary"),
        name="ada",
    )(c_all, w_ada, b_ada.reshape(depth, 1, n))


def _mod_spec(mod3, chunk, d, decode, tiles_per_seq):
    rows = mod3.shape[1]
    if decode:
        return pl.BlockSpec((None, rows, d), lambda i, *_: (0, 0, chunk))
    return pl.BlockSpec((None, rows, d), lambda i, *_: (i // tiles_per_seq, 0, chunk))


def _rope_tables(pos):
    half = ROT_DIM // 2
    inv = ROPE_THETA ** (-jnp.arange(0, ROT_DIM, 2, dtype=F32) / ROT_DIM)
    ang = pos.astype(F32)[:, None] * inv[None, :]
    cos, sin = jnp.cos(ang), jnp.sin(ang)
    r = pos.shape[0]
    ones = jnp.ones((r, HEAD_DIM - ROT_DIM), F32)
    zeros = jnp.zeros((r, HEAD_DIM - ROT_DIM), F32)
    zhalf = jnp.zeros((r, half), F32)
    c = jnp.concatenate([cos, cos, ones], axis=1)
    s_lo = jnp.concatenate([zhalf, sin, zeros], axis=1)
    s_hi = jnp.concatenate([-sin, zhalf, zeros], axis=1)
    return c, s_lo, s_hi


PROJ_ROW_CHUNK = 128
PROJ_SECTION = 1024


def _proj_kernel(x_ref, sc_ref, sh_ref, w_ref, c_ref, slo_ref, shi_ref, o_ref, k_ref, v_ref, h_scr):
    j = pl.program_id(1)

    @pl.when(j == 0)
    def _():
        h_scr[...] = (x_ref[...] * (1.0 + sc_ref[...]) + sh_ref[...]).astype(BF16)

    rows, width = o_ref.shape
    chunk = min(rows, PROJ_ROW_CHUNK)

    def row_chunks(epilogue):
        for r in range(0, rows, chunk):
            rs = slice(r, r + chunk)
            epilogue(rs, _dot(h_scr[rs, :], w_ref[...]))

    tiles = PROJ_SECTION // width
    heads_per_tile = width // HEAD_DIM

    def per_head(ref, rs, val, tile):
        for h in range(heads_per_tile):
            ref[rs, tile * heads_per_tile + h, :] = val[:, h * HEAD_DIM:(h + 1) * HEAD_DIM]

    def rope(rs, acc):
        reps = width // HEAD_DIM
        c = jnp.concatenate([c_ref[rs, :]] * reps, axis=1)
        s_lo = jnp.concatenate([slo_ref[rs, :]] * reps, axis=1)
        s_hi = jnp.concatenate([shi_ref[rs, :]] * reps, axis=1)
        half = ROT_DIM // 2
        return acc * c + pltpu.roll(acc, half, 1) * s_lo + pltpu.roll(acc, width - half, 1) * s_hi

    def store(rs, val):
        o_ref[rs, :] = val

    def store_k(tile):
        def epilogue(rs, acc):
            val = rope(rs, acc)
            o_ref[rs, :] = val
            per_head(k_ref, rs, val, tile)
        return epilogue

    def store_v(tile):
        def epilogue(rs, acc):
            o_ref[rs, :] = acc
            per_head(v_ref, rs, acc, tile)
        return epilogue

    pl.when(j < 2 * tiles)(lambda: row_chunks(lambda rs, acc: store(rs, _gelu(acc))))
    pl.when((j >= 2 * tiles) & (j < 3 * tiles))(
        lambda: row_chunks(lambda rs, acc: store(rs, rope(rs, acc))))
    for tile in range(tiles):
        pl.when(j == 3 * tiles + tile)(functools.partial(row_chunks, store_k(tile)))
        pl.when(j == 4 * tiles + tile)(functools.partial(row_chunks, store_v(tile)))
    pl.when(j >= 5 * tiles)(lambda: row_chunks(lambda rs, acc: store(rs, _sigmoid(acc))))


def _proj(x, mod3, w_in_b, tabs, *, decode, seq_len):
    m, d = x.shape
    n = w_in_b.shape[1]
    assert n == 9 * PROJ_SECTION and d == 2 * PROJ_SECTION
    bn = PROJ_SECTION if decode else PROJ_SECTION // 2
    bm = m if decode else min(1024, seq_len)
    tps = 1 if decode else seq_len // bm
    if decode:
        tab_spec = pl.BlockSpec((bm, HEAD_DIM), lambda i, j: (0, 0))
    else:
        tab_spec = pl.BlockSpec((bm, HEAD_DIM), lambda i, j: (i % tps, 0))
    kv_shape = jax.ShapeDtypeStruct((m, N_HEADS, HEAD_DIM), F32)
    kv_spec = pl.BlockSpec((bm, N_HEADS, HEAD_DIM), lambda i, j: (i, 0, 0))
    return pl.pallas_call(
        _proj_kernel,
        out_shape=(jax.ShapeDtypeStruct((m, n), F32), kv_shape, kv_shape),
        grid=(m // bm, n // bn),
        in_specs=[
            pl.BlockSpec((bm, d), lambda i, j: (i, 0)),
            _mod_spec(mod3, 1, d, decode, tps),
            _mod_spec(mod3, 0, d, decode, tps),
            pl.BlockSpec((d, bn), lambda i, j: (0, j)),
            tab_spec, tab_spec, tab_spec,
        ],
        out_specs=(pl.BlockSpec((bm, bn), lambda i, j: (i, j)), kv_spec, kv_spec),
        scratch_shapes=[pltpu.VMEM((bm, d), BF16)],
        compiler_params=_params("arbitrary", "arbitrary"),
        name="proj_dec" if decode else "proj",
    )(x, mod3, mod3, w_in_b, *tabs)


MOBA_HEADS_PER_STEP = 2


MASKED = -1e30
LOG2_E = 1.4426950408889634


def _moba_kernel(q_ref, k_ref, v_ref, o_ref, ka_scr, qa_scr, vt_scr, km_scr, s_scr, *, nb, nbp, hp):
    qi = pl.program_id(2)
    blk = MOBA_BLOCK
    pair = 2 * blk
    t_len = nb * blk
    heads = range(hp)
    cols = [slice(h * HEAD_DIM, (h + 1) * HEAD_DIM) for h in heads]
    lane = lax.broadcasted_iota(jnp.int32, (blk, HEAD_DIM), 1)

    @pl.when(qi == 0)
    def _():
        ones_row = jnp.where(lax.broadcasted_iota(jnp.int32, (16, pair), 0) == 0, 1.0, 0.0).astype(BF16)
        for h in heads:
            for j in range(nb + 2):
                rows = slice(j * blk, (j + 1) * blk)
                if j < nb:
                    kj = k_ref[rows, cols[h]]
                    km_scr[h, j:j + 1, :] = jnp.sum(kj, axis=0, keepdims=True) * (1.0 / blk)
                    ka_scr[h, rows, 0:HEAD_DIM] = kj.astype(BF16)
                else:
                    ka_scr[h, rows, 0:HEAD_DIM] = jnp.zeros((blk, HEAD_DIM), BF16)
                ka_scr[h, rows, HEAD_DIM:] = jnp.where(lane == min(j, nb), 1.0, 0.0).astype(BF16)
            for j in range(nb // 2):
                vt_scr[h, j, 0:HEAD_DIM, :] = v_ref[j * pair:(j + 1) * pair, cols[h]].T.astype(BF16)
                vt_scr[h, j, HEAD_DIM:, :] = ones_row
            qa_scr[h, HEAD_DIM + nbp:, :] = jnp.zeros((HEAD_DIM - nbp, blk), BF16)

    n_idx = lax.broadcasted_iota(jnp.int32, (nb, blk), 0)
    for h in heads:
        q = q_ref[:, cols[h]]
        qa_scr[h, 0:HEAD_DIM, :] = (q * (HEAD_DIM ** -0.5 * LOG2_E)).T.astype(BF16)
        score = _dot3(km_scr[h], (q * (HEAD_DIM ** -0.5)).T)
        cnt = jnp.zeros((nb, blk), F32)
        for m in range(nb):
            row = score[m:m + 1, :]
            beats = jnp.where(row > score, 1.0, jnp.where(row == score, jnp.where(m < n_idx, 1.0, 0.0), 0.0))
            cnt = cnt + jnp.where(m < qi, beats, 0.0)
        past = jnp.where(cnt < MOBA_TOPK, 0.0, MASKED)
        bias = jnp.where(n_idx < qi, past, jnp.where(n_idx == qi, 0.0, MASKED))
        bias = jnp.concatenate([bias, jnp.full((nbp - nb, blk), MASKED, F32)], axis=0)
        qa_scr[h, HEAD_DIM:HEAD_DIM + nbp, :] = bias.astype(BF16)

    def pair_scores(h, t):
        return _dot(ka_scr[h, pl.ds(pl.multiple_of(t * pair, pair), pair), :], qa_scr[h])

    def online_step(state, h, slot, t):
        m_run, acc = state
        s = s_scr[slot, h]
        m_new = jnp.maximum(m_run, jnp.max(s, axis=0, keepdims=True))
        p = jnp.exp2(s - m_new).astype(BF16)
        return [m_new, jnp.exp2(m_run - m_new) * acc + _dot(vt_scr[h, t], p)]

    t_own = lax.shift_right_logical(qi, 1)
    last = jnp.maximum(t_own - 1, 0)
    rel = lax.broadcasted_iota(jnp.int32, (pair, blk), 0) - (qi - 2 * t_own) * blk
    qry_i = lax.broadcasted_iota(jnp.int32, (pair, blk), 1)
    for h in heads:
        s = pair_scores(h, t_own)
        s_scr[0, h] = jnp.where(rel > qry_i, jnp.where(rel < blk, MASKED, s), s)

    def body(u, carry):
        t1 = 2 * u
        k1 = jnp.where(t1 < t_own, t1, nb // 2)
        v0 = jnp.where(u == 0, t_own, t1 - 1)
        v1 = jnp.minimum(t1, last)
        k2 = jnp.minimum(t1 + 1, last)
        state = [list(carry[2 * h:2 * h + 2]) for h in heads]
        for h in heads:
            s_scr[1, h] = pair_scores(h, k1)
        for h in heads:
            state[h] = online_step(state[h], h, 0, v0)
        for h in heads:
            s_scr[0, h] = pair_scores(h, k2)
        for h in heads:
            state[h] = online_step(state[h], h, 1, v1)
        return tuple(x for st in state for x in st)

    init = [jnp.full((1, blk), MASKED, F32), jnp.zeros((HEAD_DIM + 16, blk), F32)] * hp
    fin = lax.fori_loop(0, lax.shift_right_logical(t_own + 2, 1), body, tuple(init))
    for h in heads:
        acc = fin[2 * h + 1]
        o_ref[:, cols[h]] = (acc[0:HEAD_DIM, :] / acc[HEAD_DIM:HEAD_DIM + 1, :]).T


def _moba_prefill(proj3):
    b, t, _ = proj3.shape
    blk = MOBA_BLOCK
    nb = t // blk
    hp = MOBA_HEADS_PER_STEP
    assert t % blk == 0 and nb % 8 == 0 and N_HEADS % hp == 0
    w = hp * HEAD_DIM
    q_col, k_col, v_col = (c * HEAD_DIM // w for c in (16, 24, 32))
    nbp = -(-(nb + 1) // 16) * 16
    assert nbp <= HEAD_DIM
    return pl.pallas_call(
        functools.partial(_moba_kernel, nb=nb, nbp=nbp, hp=hp),
        out_shape=jax.ShapeDtypeStruct((b, t, N_HEADS * HEAD_DIM), F32),
        grid=(b, N_HEADS // hp, nb),
        in_specs=[
            pl.BlockSpec((None, blk, w), lambda bi, h, qi: (bi, qi, q_col + h)),
            pl.BlockSpec((None, t, w), lambda bi, h, qi: (bi, 0, k_col + h)),
            pl.BlockSpec((None, t, w), lambda bi, h, qi: (bi, 0, v_col + h)),
        ],
        out_specs=pl.BlockSpec((None, blk, w), lambda bi, h, qi: (bi, qi, h)),
        scratch_shapes=[
            pltpu.VMEM((hp, t + 2 * blk, 2 * HEAD_DIM), BF16),
            pltpu.VMEM((hp, 2 * HEAD_DIM, blk), BF16),
            pltpu.VMEM((hp, nb // 2, HEAD_DIM + 16, 2 * blk), BF16),
            pltpu.VMEM((hp, nb, HEAD_DIM), F32),
            pltpu.VMEM((2, hp, 2 * blk, blk), F32),
        ],
        compiler_params=_params("arbitrary", "arbitrary", "arbitrary"),
        name="moba",
    )(proj3, proj3, proj3)


def _paged_kmean_kernel(pt_ref, *refs, ppb):
    del pt_ref
    pages, o_ref = refs[:ppb], refs[ppb]
    for b in range(ppb // 2):
        tot = jnp.sum(pages[2 * b][...], axis=0) + jnp.sum(pages[2 * b + 1][...], axis=0)
        o_ref[b] = tot * (1.0 / MOBA_BLOCK)


def _paged_kmean(cache_k, page_table, layer):
    _, _, page, nh, dh = cache_k.shape
    s, n_pages = page_table.shape
    assert 2 * page == MOBA_BLOCK
    ppb = 16 if n_pages % 16 == 0 else 2
    page_specs = [
        pl.BlockSpec((None, None, page, nh, dh),
                     lambda si, c, pt, i=i: (layer, pt[si, c * ppb + i], 0, 0, 0))
        for i in range(ppb)
    ]
    return pl.pallas_call(
        functools.partial(_paged_kmean_kernel, ppb=ppb),
        out_shape=jax.ShapeDtypeStruct((s, n_pages // 2, nh, dh), F32),
        grid_spec=pltpu.PrefetchScalarGridSpec(
            num_scalar_prefetch=1,
            grid=(s, n_pages // ppb),
            in_specs=page_specs,
            out_specs=pl.BlockSpec((None, ppb // 2, nh, dh), lambda si, c, pt: (si, c, 0, 0)),
        ),
        compiler_params=_params("arbitrary", "arbitrary"),
        name="paged_kmean",
    )(page_table, *([cache_k] * ppb))


def _dec_select_kernel(km_ref, q_ref, o_ref, *, n_blocks):
    si = pl.program_id(0)
    q = q_ref[pl.ds(si, 1), :] * (HEAD_DIM ** -0.5)
    lane = lax.broadcasted_iota(jnp.int32, (8, LANES), 1).astype(F32)
    for h in range(N_HEADS):
        qh = jnp.broadcast_to(q[:, h * HEAD_DIM:(h + 1) * HEAD_DIM], (8, HEAD_DIM))
        sc = _dot3(qh, km_ref[h], _NT)
        sc = jnp.where(lane < n_blocks, sc, NEG_INF)
        row = jnp.zeros((8, LANES), F32)
        for r in range(MOBA_TOPK):
            best = jnp.max(sc, axis=1, keepdims=True)
            idx = jnp.min(jnp.where(sc == best, lane, 1e9), axis=1, keepdims=True)
            row = jnp.where(lane == r, idx, row)
            sc = jnp.where(lane == idx, NEG_INF, sc)
        o_ref[pl.ds(h, 1), :] = row[0:1, :].astype(jnp.int32)


def _dec_select(kmean, proj_s):
    s, n_blocks, nh, dh = kmean.shape
    assert MOBA_TOPK <= n_blocks <= LANES
    km = jnp.transpose(kmean, (0, 2, 1, 3))
    km = jnp.pad(km, ((0, 0), (0, 0), (0, LANES - n_blocks), (0, 0)))
    return pl.pallas_call(
        functools.partial(_dec_select_kernel, n_blocks=n_blocks),
        out_shape=jax.ShapeDtypeStruct((s, nh, LANES), jnp.int32),
        grid=(s,),
        in_specs=[
            pl.BlockSpec((None, nh, LANES, dh), lambda si: (si, 0, 0, 0)),
            pl.BlockSpec((s, nh * dh), lambda si: (0, 2)),
        ],
        out_specs=pl.BlockSpec((None, nh, LANES), lambda si: (si, 0, 0)),
        compiler_params=_params("arbitrary"),
        name="dec_select",
    )(km, proj_s)


def _dec_attn_kernel(sel_ref, pt_ref, ck_hbm, cv_hbm, q_ref, kn_ref, vn_ref, o_ref,
                     k_scr, v_scr, k_sem, v_sem, *, layer):
    si = pl.program_id(0)
    n_sel = 2 * MOBA_TOPK

    def slab_copies(h):
        for i in range(n_sel):
            j, r = divmod(i, 2)
            pid = pt_ref[si, 2 * sel_ref[(si * N_HEADS + h) * MOBA_TOPK + j] + r]
            yield pltpu.make_async_copy(ck_hbm.at[layer, pid, :, h, :], k_scr.at[h * n_sel + i], k_sem)
            yield pltpu.make_async_copy(cv_hbm.at[layer, pid, :, h, :], v_scr.at[h * n_sel + i], v_sem)

    for h in range(N_HEADS):
        for cp in slab_copies(h):
            cp.start()
    n_slabs = k_scr.shape[0]
    pltpu.make_async_copy(ck_hbm.at[layer, pl.ds(0, n_slabs), :, 0, :], k_scr, k_sem).wait()
    pltpu.make_async_copy(cv_hbm.at[layer, pl.ds(0, n_slabs), :, 0, :], v_scr, v_sem).wait()

    for h in range(N_HEADS):
        q = q_ref[h:h + 1, :] * (HEAD_DIM ** -0.5)
        s_own = jnp.sum(q * kn_ref[h:h + 1, :], axis=-1, keepdims=True)
        s_sel = [jnp.sum(k_scr[h * n_sel + i] * q, axis=-1, keepdims=True) for i in range(n_sel)]
        m = s_own
        for s_i in s_sel:
            m = jnp.maximum(m, jnp.max(s_i, axis=0, keepdims=True))
        p_own = jnp.exp(s_own - m)
        denom = p_own
        out = p_own * vn_ref[h:h + 1, :]
        for i in range(n_sel):
            p = jnp.exp(s_sel[i] - m)
            denom = denom + jnp.sum(p, axis=0, keepdims=True)
            out = out + jnp.sum(p * v_scr[h * n_sel + i], axis=0, keepdims=True)
        o_ref[h:h + 1, :] = out / denom


def _dec_attn(cache_k, cache_v, page_table, sel, q3, k3, v3, layer):
    _, n_phys, page, nh, dh = cache_k.shape
    s = page_table.shape[0]
    sel_flat = sel[:, :, :MOBA_TOPK].reshape(-1)
    n_slabs = nh * 2 * MOBA_TOPK
    assert nh == N_HEADS and n_phys >= n_slabs
    row_spec = pl.BlockSpec((None, nh, dh), lambda si, sel_r, pt_r: (si, 0, 0))
    return pl.pallas_call(
        functools.partial(_dec_attn_kernel, layer=layer),
        out_shape=jax.ShapeDtypeStruct((s, nh, dh), F32),
        grid_spec=pltpu.PrefetchScalarGridSpec(
            num_scalar_prefetch=2,
            grid=(s,),
            in_specs=[pl.BlockSpec(memory_space=pl.ANY)] * 2 + [row_spec] * 3,
            out_specs=row_spec,
            scratch_shapes=[pltpu.VMEM((n_slabs, page, dh), F32), pltpu.VMEM((n_slabs, page, dh), F32),
                            pltpu.SemaphoreType.DMA, pltpu.SemaphoreType.DMA],
        ),
        compiler_params=_params("arbitrary"),
        name="dec_attn",
    )(sel_flat, page_table, cache_k, cache_v, q3, k3, v3)


def _mix_kernel(u_ref, va_ref, attn_ref, ga0_ref, ga1_ref, gb0_ref, gb1_ref, x_ref, g1_ref,
                ws_ref, bs_ref, lvg_ref, lvb_ref, wba_ref, wbb_ref, wo_ref, l1g_ref, l1b_ref,
                *out_refs, decode, alpha):
    vn = _layer_norm(va_ref[...], lvg_ref[...], lvb_ref[...])
    if decode:
        out_refs[1][...] = vn
        mixed = vn * ws_ref[...] + bs_ref[...]
    else:
        rows = vn.shape[0]
        vnb = vn.astype(BF16)
        r_i = lax.broadcasted_iota(jnp.int32, (CHUNK, CHUNK), 0)
        c_i = lax.broadcasted_iota(jnp.int32, (CHUNK, CHUNK), 1)
        w_tril = [jnp.where(c_i <= r_i, ws_ref[g], 0.0).astype(BF16) for g in range(SGU_GROUPS)]
        gd = vn.shape[1] // SGU_GROUPS
        chunks = []
        for c in range(rows // CHUNK):
            cols = []
            for g in range(SGU_GROUPS):
                v_cg = vnb[c * CHUNK:(c + 1) * CHUNK, g * gd:(g + 1) * gd]
                cols.append(_dot(w_tril[g], v_cg) + bs_ref[:, g:g + 1])
            chunks.append(jnp.concatenate(cols, axis=1))
        mixed = jnp.concatenate(chunks, axis=0)
    a_pre = (u_ref[...] * mixed).astype(BF16)
    a_out = _dot(a_pre, wba_ref[...])
    b_out = _dot(attn_ref[...].astype(BF16), wbb_ref[...])
    g_a = jnp.concatenate([ga0_ref[...], ga1_ref[...]], axis=1)
    g_b = jnp.concatenate([gb0_ref[...], gb1_ref[...]], axis=1)
    merged = (g_a * a_out + g_b * b_out).astype(BF16)
    y = _dot(merged, wo_ref[...])
    out_refs[0][...] = _layer_norm(alpha * x_ref[...] + (1.0 + g1_ref[...]) * y, l1g_ref[...], l1b_ref[...])


def _mix(proj, attn, x, mod3, w_s, b_s, ln_v_g, ln_v_b, w_ba_b, w_bb_b, w_o_b, ln1_g, ln1_b,
         *, decode, seq_len, alpha):
    m, d = x.shape
    sw = w_ba_b.shape[0]
    bm = m if decode else min(256, seq_len)
    tps = 1 if decode else seq_len // bm
    const = lambda *_: (0, 0)
    single = pl.Buffered(1)

    def col(cidx):
        return pl.BlockSpec((bm, sw), lambda i: (i, cidx))

    if decode:
        gd = sw // SGU_GROUPS
        ws_arg = jnp.repeat(w_s[:, 0, 0], gd)[None, :]
        bs_arg = jnp.repeat(b_s[:, 0], gd)[None, :]
        ws_spec = pl.BlockSpec((1, sw), const)
        bs_spec = pl.BlockSpec((1, sw), const)
        out_shape = (jax.ShapeDtypeStruct((m, d), F32), jax.ShapeDtypeStruct((m, sw), F32))
        out_specs = (pl.BlockSpec((bm, d), lambda i: (i, 0)), pl.BlockSpec((bm, sw), lambda i: (i, 0)))
    else:
        ws_arg, bs_arg = w_s, b_s.T
        ws_spec = pl.BlockSpec(w_s.shape, lambda i: (0, 0, 0))
        bs_spec = pl.BlockSpec((CHUNK, SGU_GROUPS), const)
        out_shape = (jax.ShapeDtypeStruct((m, d), F32),)
        out_specs = (pl.BlockSpec((bm, d), lambda i: (i, 0)),)

    return pl.pallas_call(
        functools.partial(_mix_kernel, decode=decode, alpha=alpha),
        out_shape=out_shape,
        grid=(m // bm,),
        in_specs=[
            col(0), col(1),
            pl.BlockSpec((bm, sw), lambda i: (i, 0)),
            col(5), col(6), col(7), col(8),
            pl.BlockSpec((bm, d), lambda i: (i, 0)),
            _mod_spec(mod3, 2, d, decode, tps),
            ws_spec, bs_spec,
            pl.BlockSpec((1, sw), const), pl.BlockSpec((1, sw), const),
            pl.BlockSpec((sw, d), const, pipeline_mode=single),
            pl.BlockSpec((sw, d), const, pipeline_mode=single),
            pl.BlockSpec((d, d), const, pipeline_mode=single),
            pl.BlockSpec((1, d), const), pl.BlockSpec((1, d), const),
        ],
        out_specs=out_specs,
        compiler_params=_params("arbitrary"),
        name="mix_dec" if decode else "mix",
    )(proj, proj, attn, proj, proj, proj, proj, x, mod3, ws_arg, bs_arg,
      ln_v_g[None, :], ln_v_b[None, :], w_ba_b, w_bb_b, w_o_b, ln1_g[None, :], ln1_b[None, :])


def _ffn_kernel(x_ref, sc_ref, sh_ref, g_ref, wg_ref, wu_ref, wd_ref, lg_ref, lb_ref, o_ref,
                h_scr, acc_scr, *, alpha):
    f = pl.program_id(1)

    @pl.when(f == 0)
    def _():
        h_scr[...] = (x_ref[...] * (1.0 + sc_ref[...]) + sh_ref[...]).astype(BF16)
        acc_scr[...] = jnp.zeros_like(acc_scr)

    h = h_scr[...]
    a = _dot(h, wg_ref[...])
    act = (a * _sigmoid(a) * _dot(h, wu_ref[...])).astype(BF16)
    acc_scr[...] += _dot(act, wd_ref[...])

    @pl.when(f == pl.num_programs(1) - 1)
    def _():
        y = alpha * x_ref[...] + (1.0 + g_ref[...]) * acc_scr[...]
        o_ref[...] = _layer_norm(y, lg_ref[...], lb_ref[...])


def _ffn(x, mod3, wg_b, wu_b, wd_b, ln_g, ln_b, *, decode, seq_len, alpha):
    m, d = x.shape
    ff = wg_b.shape[1]
    bf = 512 if ff % 512 == 0 else ff
    bm = m if decode else min(512, seq_len)
    tps = 1 if decode else seq_len // bm
    const = lambda *_: (0, 0)
    return pl.pallas_call(
        functools.partial(_ffn_kernel, alpha=alpha),
        out_shape=jax.ShapeDtypeStruct((m, d), F32),
        grid=(m // bm, ff // bf),
        in_specs=[
            pl.BlockSpec((bm, d), lambda i, f: (i, 0)),
            _mod_spec(mod3, 4, d, decode, tps),
            _mod_spec(mod3, 3, d, decode, tps),
            _mod_spec(mod3, 5, d, decode, tps),
            pl.BlockSpec((d, bf), lambda i, f: (0, f)),
            pl.BlockSpec((d, bf), lambda i, f: (0, f)),
            pl.BlockSpec((bf, d), lambda i, f: (f, 0)),
            pl.BlockSpec((1, d), const), pl.BlockSpec((1, d), const),
        ],
        out_specs=pl.BlockSpec((bm, d), lambda i, f: (i, 0)),
        scratch_shapes=[pltpu.VMEM((bm, d), BF16), pltpu.VMEM((bm, d), F32)],
        compiler_params=_params("arbitrary", "arbitrary"),
        name="ffn_dec" if decode else "ffn",
    )(x, mod3, mod3, mod3, wg_b, wu_b, wd_b, ln_g[None, :], ln_b[None, :])


def _moe_route_kernel(x_ref, sc_ref, sh_ref, r_ref, cnt0_ref, h_ref, route_ref, cnt_ref, cnt_scr):
    i = pl.program_id(0)
    rows = x_ref.shape[0]

    @pl.when(i == 0)
    def _():
        cnt_scr[...] = cnt0_ref[...]

    h = x_ref[...] * (1.0 + sc_ref[...]) + sh_ref[...]
    h_ref[...] = h
    lane = lax.broadcasted_iota(jnp.int32, (rows, LANES), 1)
    lane_f = lane.astype(F32)
    logits = jnp.where(lane < N_EXPERTS, _dot3(h, r_ref[...]), NEG_INF)
    v1 = jnp.max(logits, axis=1, keepdims=True)
    i1 = jnp.min(jnp.where(logits == v1, lane_f, 1e9), axis=1, keepdims=True)
    rest = jnp.where(lane_f == i1, NEG_INF, logits)
    v2 = jnp.max(rest, axis=1, keepdims=True)
    i2 = jnp.min(jnp.where(rest == v2, lane_f, 1e9), axis=1, keepdims=True)
    ex = jnp.exp(v2 - v1)
    w1 = 1.0 / (1.0 + ex)
    w2 = ex / (1.0 + ex)
    picks = jnp.where(lane_f == i1, 1.0, jnp.where(lane_f == i2, 1.0, 0.0))
    if rows >= LANES:
        r_i = lax.broadcasted_iota(jnp.int32, (rows, rows), 0)
        c_i = lax.broadcasted_iota(jnp.int32, (rows, rows), 1)
        before = _dot(jnp.where(c_i < r_i, 1.0, 0.0).astype(BF16), picks.astype(BF16))
    else:
        r_i = lax.broadcasted_iota(jnp.int32, (rows, LANES), 0)
        before = jnp.zeros((rows, LANES), F32)
        for t in range(rows - 1):
            before = before + jnp.where(r_i > t, picks[t:t + 1, :], 0.0)
    before = before + cnt_scr[0:1, :]
    rank1 = jnp.sum(jnp.where(lane_f == i1, before, 0.0), axis=1, keepdims=True)
    rank2 = jnp.sum(jnp.where(lane_f == i2, before, 0.0), axis=1, keepdims=True)
    rec = jnp.zeros((rows, LANES), F32)
    for k, val in enumerate((i1, i2, w1, w2, rank1, rank2)):
        rec = jnp.where(lane == k, val, rec)
    route_ref[...] = rec
    cnt_scr[...] = cnt_scr[...] + jnp.sum(picks, axis=0, keepdims=True)
    cnt_ref[...] = cnt_scr[...]


def _moe_route(x, mod3, router, cnt0, *, decode, seq_len):
    m, d = x.shape
    bm = m if decode else min(512, seq_len)
    tps = 1 if decode else seq_len // bm
    router_p = jnp.pad(router, ((0, 0), (0, LANES - router.shape[1])))
    return pl.pallas_call(
        _moe_route_kernel,
        out_shape=(jax.ShapeDtypeStruct((m, d), F32), jax.ShapeDtypeStruct((m, LANES), F32),
                   jax.ShapeDtypeStruct((8, LANES), F32)),
        grid=(m // bm,),
        in_specs=[
            pl.BlockSpec((bm, d), lambda i: (i, 0)),
            _mod_spec(mod3, 4, d, decode, tps),
            _mod_spec(mod3, 3, d, decode, tps),
            pl.BlockSpec((d, LANES), lambda i: (0, 0)),
            pl.BlockSpec((8, LANES), lambda i: (0, 0)),
        ],
        out_specs=(pl.BlockSpec((bm, d), lambda i: (i, 0)), pl.BlockSpec((bm, LANES), lambda i: (i, 0)),
                   pl.BlockSpec((8, LANES), lambda i: (0, 0))),
        scratch_shapes=[pltpu.VMEM((8, LANES), F32)],
        compiler_params=_params("arbitrary"),
        name="moe_route_dec" if decode else "moe_route",
    )(x, mod3, mod3, router_p, cnt0)


def _row_gather_start(src_hbm, idx_ref, base, dst, sem):
    def issue(r, carry):
        pltpu.make_async_copy(src_hbm.at[pl.ds(idx_ref[base + r], 1), :], dst.at[pl.ds(r, 1), :], sem).start()
        return carry

    lax.fori_loop(0, dst.shape[0], issue, 0, unroll=8)


def _row_gather_wait(src_hbm, dst, sem):
    pltpu.make_async_copy(src_hbm.at[pl.ds(0, dst.shape[0]), :], dst, sem).wait()


MOE_HALF_ROWS = 512


def _moe_group_kernel(te_ref, nh_ref, tok_ref, h_hbm, wg_ref, wu_ref, wd_ref, y_ref, xg_scr, hb_scr, sem):
    del te_ref
    i = pl.program_id(0)
    f = pl.program_id(1)
    half = xg_scr.shape[0]
    n_halves = nh_ref[i]

    @pl.when(f == 0)
    def _():
        y_ref[...] = jnp.zeros_like(y_ref)

    for hf in range(2):
        @pl.when((f == 0) & (n_halves > hf))
        def _(hf=hf):
            _row_gather_start(h_hbm, tok_ref, (2 * i + hf) * half, xg_scr, sem)
            _row_gather_wait(h_hbm, xg_scr, sem)
            hb_scr[hf] = xg_scr[...].astype(BF16)

    def compute(halves):
        wg, wu, wd = (w[...].astype(BF16) for w in (wg_ref, wu_ref, wd_ref))
        for hf in range(halves):
            h = hb_scr[hf]
            a = _dot(h, wg)
            act = (a * _sigmoid(a) * _dot(h, wu)).astype(BF16)
            y_ref[hf * half:(hf + 1) * half, :] += _dot(act, wd)

    pl.when(n_halves == 1)(lambda: compute(1))
    pl.when(n_halves == 2)(lambda: compute(2))


def _moe_group(h_all, tile_expert, tile_halves, tok_of_slot, wg, wu, wd):
    d = h_all.shape[1]
    ff = wg.shape[2]
    bf = 512 if ff % 512 == 0 else ff
    nf = ff // bf
    half = MOE_HALF_ROWS
    n_tiles = tile_expert.shape[0]

    def f_idx(i, f, nh):
        return jnp.where(nh[i] > 0, f, nf - 1)

    return pl.pallas_call(
        _moe_group_kernel,
        out_shape=jax.ShapeDtypeStruct((n_tiles * 2 * half, d), F32),
        grid_spec=pltpu.PrefetchScalarGridSpec(
            num_scalar_prefetch=3,
            grid=(n_tiles, nf),
            in_specs=[
                pl.BlockSpec(memory_space=pl.ANY),
                pl.BlockSpec((None, d, bf), lambda i, f, te, nh, tok: (te[i], 0, f_idx(i, f, nh))),
                pl.BlockSpec((None, d, bf), lambda i, f, te, nh, tok: (te[i], 0, f_idx(i, f, nh))),
                pl.BlockSpec((None, bf, d), lambda i, f, te, nh, tok: (te[i], f_idx(i, f, nh), 0)),
            ],
            out_specs=pl.BlockSpec((2 * half, d), lambda i, f, te, nh, tok: (i, 0)),
            scratch_shapes=[pltpu.VMEM((half, d), F32), pltpu.VMEM((2, half, d), BF16),
                            pltpu.SemaphoreType.DMA],
        ),
        compiler_params=_params("arbitrary", "arbitrary"),
        name="moe_group",
    )(tile_expert, tile_halves, tok_of_slot, h_all, wg, wu, wd)


def _moe_combine_kernel(s1_ref, s2_ref, y_hbm, route_ref, x_ref, g_ref, lg_ref, lb_ref, o_ref,
                        y1_scr, y2_scr, sem1, sem2, *, alpha):
    i = pl.program_id(0)
    rows = x_ref.shape[0]
    _row_gather_start(y_hbm, s1_ref, i * rows, y1_scr, sem1)
    _row_gather_start(y_hbm, s2_ref, i * rows, y2_scr, sem2)
    _row_gather_wait(y_hbm, y1_scr, sem1)
    _row_gather_wait(y_hbm, y2_scr, sem2)
    route = route_ref[...]
    f = route[:, 2:3] * y1_scr[...] + route[:, 3:4] * y2_scr[...]
    y = alpha * x_ref[...] + (1.0 + g_ref[...]) * f
    o_ref[...] = _layer_norm(y, lg_ref[...], lb_ref[...])


def _moe_combine(y_sorted, slot1, slot2, route, x, mod3, ln_g, ln_b, *, decode, seq_len, alpha):
    m, d = x.shape
    bm = m if decode else min(256, seq_len)
    tps = 1 if decode else seq_len // bm
    rows_mod = mod3.shape[1]
    if decode:
        g_spec = pl.BlockSpec((None, rows_mod, d), lambda i, s1, s2: (0, 0, 5))
    else:
        g_spec = pl.BlockSpec((None, rows_mod, d), lambda i, s1, s2: (i // tps, 0, 5))
    const = lambda i, s1, s2: (0, 0)
    return pl.pallas_call(
        functools.partial(_moe_combine_kernel, alpha=alpha),
        out_shape=jax.ShapeDtypeStruct((m, d), F32),
        grid_spec=pltpu.PrefetchScalarGridSpec(
            num_scalar_prefetch=2,
            grid=(m // bm,),
            in_specs=[
                pl.BlockSpec(memory_space=pl.ANY),
                pl.BlockSpec((bm, LANES), lambda i, s1, s2: (i, 0)),
                pl.BlockSpec((bm, d), lambda i, s1, s2: (i, 0)),
                g_spec,
                pl.BlockSpec((1, d), const), pl.BlockSpec((1, d), const),
            ],
            out_specs=pl.BlockSpec((bm, d), lambda i, s1, s2: (i, 0)),
            scratch_shapes=[pltpu.VMEM((bm, d), F32), pltpu.VMEM((bm, d), F32),
                            pltpu.SemaphoreType.DMA, pltpu.SemaphoreType.DMA],
        ),
        compiler_params=_params("arbitrary"),
        name="moe_combine_dec" if decode else "moe_combine",
    )(slot1, slot2, y_sorted, route, x, mod3, ln_g[None, :], ln_b[None, :])


def _moe(xp, xs, mod_p, mod_s, router, wg, wu, wd, ln_g, ln_b, *, seq_len, alpha):
    mp, ms = xp.shape[0], xs.shape[0]
    m = mp + ms
    ne = wg.shape[0]
    half = MOE_HALF_ROWS
    n_tiles = -(-m // half) + ne
    h_p, route_p, cnt_p = _moe_route(xp, mod_p, router, jnp.zeros((8, LANES), F32), decode=False, seq_len=seq_len)
    h_s, route_s, cnt = _moe_route(xs, mod_s, router, cnt_p, decode=True, seq_len=1)
    h_all = jnp.concatenate([h_p, h_s], axis=0)
    route = jnp.concatenate([route_p, route_s], axis=0)

    counts = cnt[0, :ne].astype(jnp.int32)
    n_half = (counts + half - 1) // half
    cap = (n_half + 1) // 2 * (2 * half)
    ends = jnp.cumsum(cap)
    starts = ends - cap
    used_end = starts + n_half * half
    e1, e2 = route[:, 0].astype(jnp.int32), route[:, 1].astype(jnp.int32)
    slot1 = starts[e1] + route[:, 4].astype(jnp.int32)
    slot2 = starts[e2] + route[:, 5].astype(jnp.int32)
    tok = jnp.arange(m, dtype=jnp.int32)
    tok_of_slot = jnp.zeros((n_tiles * 2 * half,), jnp.int32).at[jnp.concatenate([slot1, slot2])].set(
        jnp.concatenate([tok, tok]))
    tile_lo = jnp.arange(n_tiles, dtype=jnp.int32) * (2 * half)
    last_lo = jnp.maximum(ends[-1] - 2 * half, 0)
    tile_expert = jnp.sum((jnp.minimum(tile_lo, last_lo)[:, None] >= ends[None, :]).astype(jnp.int32), axis=1)
    tile_expert = jnp.minimum(tile_expert, ne - 1)
    tile_halves = jnp.where(tile_lo >= ends[-1], 0, jnp.where(tile_lo + half < used_end[tile_expert], 2, 1))

    y_sorted = _moe_group(h_all, tile_expert, tile_halves.astype(jnp.int32), tok_of_slot, wg, wu, wd)
    out_p = _moe_combine(y_sorted, slot1[:mp], slot2[:mp], route_p, xp, mod_p, ln_g, ln_b,
                         decode=False, seq_len=seq_len, alpha=alpha)
    out_s = _moe_combine(y_sorted, slot1[mp:], slot2[mp:], route_s, xs, mod_s, ln_g, ln_b,
                         decode=True, seq_len=1, alpha=alpha)
    return out_p, out_s


def kernel(x_prompt, x_sample, cache_k, cache_v, page_table, c_prompt, c_sample,
           w_in, ln_v_g, ln_v_b, w_s, b_s, w_ba, w_bb, w_o, w_ada, b_ada,
           ln1_g, ln1_b, ln2_g, ln2_b, ffn_w_gate, ffn_w_up, ffn_w_down,
           moe_router, moe_w_gate, moe_w_up, moe_w_down):
    depth = w_in.shape[0]
    b, t, d = x_prompt.shape
    s = x_sample.shape[0]
    assert x_sample.shape[1] == 1 and s % 8 == 0 and b + s <= ADA_ROWS
    n_pages, page = page_table.shape[1], cache_k.shape[2]
    past_len = n_pages * page
    alpha = (2 * depth) ** 0.25
    sw = w_ba.shape[1]

    c_all = jnp.concatenate([c_prompt, c_sample, jnp.zeros((ADA_ROWS - b - s, d), F32)], axis=0)
    mod_all = _ada(c_all, w_ada, b_ada)

    tabs_p = _rope_tables(jnp.arange(t))
    tabs_s = _rope_tables(jnp.full((s,), past_len, jnp.int32))

    xp = x_prompt.reshape(b * t, d)
    xs = x_sample.reshape(s, d)
    k_p, v_p, k_s, v_s, vn_s = [], [], [], [], []
    aw = N_HEADS * HEAD_DIM
    q_lo = 2 * sw
    for l in range(depth):
        mod_p = mod_all[l, :b][:, None, :]
        mod_s = mod_all[l, b:b + s][None]
        w_in_b = w_in[l].astype(BF16)
        mix_w = (w_s[l], b_s[l], ln_v_g[l], ln_v_b[l], w_ba[l].astype(BF16), w_bb[l].astype(BF16),
                 w_o[l].astype(BF16), ln1_g[l], ln1_b[l])
        proj_p, k_new, v_new = _proj(xp, mod_p, w_in_b, tabs_p, decode=False, seq_len=t)
        attn_p = _moba_prefill(proj_p.reshape(b, t, -1)).reshape(b * t, -1)
        (xp1,) = _mix(proj_p, attn_p, xp, mod_p, *mix_w, decode=False, seq_len=t, alpha=alpha)
        k_p.append(k_new.reshape(b, t, N_HEADS, HEAD_DIM))
        v_p.append(v_new.reshape(b, t, N_HEADS, HEAD_DIM))

        proj_s, k3, v3 = _proj(xs, mod_s, w_in_b, tabs_s, decode=True, seq_len=1)
        kmean = _paged_kmean(cache_k, page_table, l)
        sel = _dec_select(kmean, proj_s)
        q3 = proj_s[:, q_lo:q_lo + aw].reshape(s, N_HEADS, HEAD_DIM)
        attn_s = _dec_attn(cache_k, cache_v, page_table, sel, q3, k3, v3, l).reshape(s, aw)
        xs1, vn = _mix(proj_s, attn_s, xs, mod_s, *mix_w, decode=True, seq_len=1, alpha=alpha)

        if l % 2 == 0:
            ffn_w = (ffn_w_gate[l // 2].astype(BF16), ffn_w_up[l // 2].astype(BF16),
                     ffn_w_down[l // 2].astype(BF16), ln2_g[l], ln2_b[l])
            xp = _ffn(xp1, mod_p, *ffn_w, decode=False, seq_len=t, alpha=alpha)
            xs = _ffn(xs1, mod_s, *ffn_w, decode=True, seq_len=1, alpha=alpha)
        else:
            xp, xs = _moe(xp1, xs1, mod_p, mod_s, moe_router[l // 2], moe_w_gate[l // 2], moe_w_up[l // 2],
                          moe_w_down[l // 2], ln2_g[l], ln2_b[l], seq_len=t, alpha=alpha)
        k_s.append(k3.reshape(s, 1, N_HEADS, HEAD_DIM))
        v_s.append(v3.reshape(s, 1, N_HEADS, HEAD_DIM))
        vn_s.append(vn.reshape(s, 1, sw))

    return (xp.reshape(b, t, d), xs.reshape(s, 1, d), jnp.stack(k_p), jnp.stack(v_p),
            jnp.stack(k_s), jnp.stack(v_s), jnp.stack(vn_s))
```

```python
import functools

import jax
import jax.numpy as jnp
from jax import lax
from jax.experimental import pallas as pl
from jax.experimental.pallas import tpu as pltpu

F32 = jnp.float32
BF16 = jnp.bfloat16

CHUNK = 128
SGU_GROUPS = 8
N_HEADS = 8
HEAD_DIM = 128
ROT_DIM = HEAD_DIM // 4
MOBA_BLOCK = 256
MOBA_TOPK = 3
ROPE_THETA = 500000.0
LN_EPS = 1e-5
N_EXPERTS = 8
LANES = 128
ADA_ROWS = 16
VMEM_LIMIT = 56 * 1024 * 1024
NEG_INF = float("-inf")

_NN = (((1,), (0,)), ((), ()))
_NT = (((1,), (1,)), ((), ()))


def _params(*sem):
    return pltpu.CompilerParams(dimension_semantics=sem, vmem_limit_bytes=VMEM_LIMIT)


def _sigmoid(x):
    return 1.0 / (1.0 + jnp.exp(-x))


def _gelu(x):
    return 0.5 * x * (1.0 + jnp.tanh(0.7978845608028654 * (x + 0.044715 * (x * x * x))))


def _layer_norm(x, g, b):
    mu = jnp.mean(x, axis=-1, keepdims=True)
    xc = x - mu
    var = jnp.mean(xc * xc, axis=-1, keepdims=True)
    return xc * lax.rsqrt(var + LN_EPS) * g + b


def _dot(a, b, dims=_NN):
    return lax.dot_general(a, b, dims, preferred_element_type=F32)


def _split_bf16(a):
    hi = a.astype(BF16)
    lo = (a - hi.astype(F32)).astype(BF16)
    return hi, lo


def _dot3(a, b, dims=_NN):
    ah, al = _split_bf16(a)
    bh, bl = _split_bf16(b)
    return _dot(ah, bh, dims) + (_dot(ah, bl, dims) + _dot(al, bh, dims))


def _ada_kernel(c_ref, w_ref, b_ref, o_ref):
    c = c_ref[...]
    a = (c * _sigmoid(c)).astype(BF16)
    o_ref[...] = _dot(a, w_ref[...].astype(BF16)) + b_ref[...]


def _ada(c_all, w_ada, b_ada):
    depth, d, n = w_ada.shape
    bn = 1024
    return pl.pallas_call(
        _ada_kernel,
        out_shape=jax.ShapeDtypeStruct((depth, ADA_ROWS, n), F32),
        grid=(depth, n // bn),
        in_specs=[
            pl.BlockSpec((ADA_ROWS, d), lambda l, j: (0, 0)),
            pl.BlockSpec((None, d, bn), lambda l, j: (l, 0, j)),
            pl.BlockSpec((None, 1, bn), lambda l, j: (l, 0, j)),
        ],
        out_specs=pl.BlockSpec((None, ADA_ROWS, bn), lambda l, j: (l, 0, j)),
        compiler_params=_params("arbitrary", "arbitrary"),
        name="ada",
    )(c_all, w_ada, b_ada.reshape(depth, 1, n))


def _mod_spec(mod3, chunk, d, decode, tiles_per_seq):
    rows = mod3.shape[1]
    if decode:
        return pl.BlockSpec((None, rows, d), lambda i, *_: (0, 0, chunk))
    return pl.BlockSpec((None, rows, d), lambda i, *_: (i // tiles_per_seq, 0, chunk))


def _rope_tables(pos):
    half = ROT_DIM // 2
    inv = ROPE_THETA ** (-jnp.arange(0, ROT_DIM, 2, dtype=F32) / ROT_DIM)
    ang = pos.astype(F32)[:, None] * inv[None, :]
    cos, sin = jnp.cos(ang), jnp.sin(ang)
    r = pos.shape[0]
    ones = jnp.ones((r, HEAD_DIM - ROT_DIM), F32)
    zeros = jnp.zeros((r, HEAD_DIM - ROT_DIM), F32)
    zhalf = jnp.zeros((r, half), F32)
    c = jnp.concatenate([cos, cos, ones], axis=1)
    s_lo = jnp.concatenate([zhalf, sin, zeros], axis=1)
    s_hi = jnp.concatenate([-sin, zhalf, zeros], axis=1)
    return c, s_lo, s_hi


PROJ_ROW_CHUNK = 128
PROJ_SECTION = 1024


def _proj_kernel(x_ref, sc_ref, sh_ref, w_ref, c_ref, slo_ref, shi_ref, o_ref, k_ref, v_ref, h_scr):
    j = pl.program_id(1)

    @pl.when(j == 0)
    def _():
        h_scr[...] = (x_ref[...] * (1.0 + sc_ref[...]) + sh_ref[...]).astype(BF16)

    rows, width = o_ref.shape
    chunk = min(rows, PROJ_ROW_CHUNK)

    def row_chunks(epilogue):
        for r in range(0, rows, chunk):
            rs = slice(r, r + chunk)
            epilogue(rs, _dot(h_scr[rs, :], w_ref[...]))

    tiles = PROJ_SECTION // width
    heads_per_tile = width // HEAD_DIM

    def per_head(ref, rs, val, tile):
        for h in range(heads_per_tile):
            ref[rs, tile * heads_per_tile + h, :] = val[:, h * HEAD_DIM:(h + 1) * HEAD_DIM]

    def rope(rs, acc):
        reps = width // HEAD_DIM
        c = jnp.concatenate([c_ref[rs, :]] * reps, axis=1)
        s_lo = jnp.concatenate([slo_ref[rs, :]] * reps, axis=1)
        s_hi = jnp.concatenate([shi_ref[rs, :]] * reps, axis=1)
        half = ROT_DIM // 2
        return acc * c + pltpu.roll(acc, half, 1) * s_lo + pltpu.roll(acc, width - half, 1) * s_hi

    def store(rs, val):
        o_ref[rs, :] = val

    def store_k(tile):
        def epilogue(rs, acc):
            val = rope(rs, acc)
            o_ref[rs, :] = val
            per_head(k_ref, rs, val, tile)
        return epilogue

    def store_v(tile):
        def epilogue(rs, acc):
            o_ref[rs, :] = acc
            per_head(v_ref, rs, acc, tile)
        return epilogue

    pl.when(j < 2 * tiles)(lambda: row_chunks(lambda rs, acc: store(rs, _gelu(acc))))
    pl.when((j >= 2 * tiles) & (j < 3 * tiles))(
        lambda: row_chunks(lambda rs, acc: store(rs, rope(rs, acc))))
    for tile in range(tiles):
        pl.when(j == 3 * tiles + tile)(functools.partial(row_chunks, store_k(tile)))
        pl.when(j == 4 * tiles + tile)(functools.partial(row_chunks, store_v(tile)))
    pl.when(j >= 5 * tiles)(lambda: row_chunks(lambda rs, acc: store(rs, _sigmoid(acc))))


def _proj(x, mod3, w_in_b, tabs, *, decode, seq_len):
    m, d = x.shape
    n = w_in_b.shape[1]
    assert n == 9 * PROJ_SECTION and d == 2 * PROJ_SECTION
    bn = PROJ_SECTION if decode else PROJ_SECTION // 2
    bm = m if decode else min(1024, seq_len)
    tps = 1 if decode else seq_len // bm
    if decode:
        tab_spec = pl.BlockSpec((bm, HEAD_DIM), lambda i, j: (0, 0))
    else:
        tab_spec = pl.BlockSpec((bm, HEAD_DIM), lambda i, j: (i % tps, 0))
    kv_shape = jax.ShapeDtypeStruct((m, N_HEADS, HEAD_DIM), F32)
    kv_spec = pl.BlockSpec((bm, N_HEADS, HEAD_DIM), lambda i, j: (i, 0, 0))
    return pl.pallas_call(
        _proj_kernel,
        out_shape=(jax.ShapeDtypeStruct((m, n), F32), kv_shape, kv_shape),
        grid=(m // bm, n // bn),
        in_specs=[
            pl.BlockSpec((bm, d), lambda i, j: (i, 0)),
            _mod_spec(mod3, 1, d, decode, tps),
            _mod_spec(mod3, 0, d, decode, tps),
            pl.BlockSpec((d, bn), lambda i, j: (0, j)),
            tab_spec, tab_spec, tab_spec,
        ],
        out_specs=(pl.BlockSpec((bm, bn), lambda i, j: (i, j)), kv_spec, kv_spec),
        scratch_shapes=[pltpu.VMEM((bm, d), BF16)],
        compiler_params=_params("arbitrary", "arbitrary"),
        name="proj_dec" if decode else "proj",
    )(x, mod3, mod3, w_in_b, *tabs)


MOBA_HEADS_PER_STEP = 2


MASKED = -1e30
LOG2_E = 1.4426950408889634


def _moba_kernel(q_ref, k_ref, v_ref, o_ref, ka_scr, qa_scr, vt_scr, km_scr, s_scr, *, nb, nbp, hp):
    qi = pl.program_id(2)
    blk = MOBA_BLOCK
    pair = 2 * blk
    t_len = nb * blk
    heads = range(hp)
    cols = [slice(h * HEAD_DIM, (h + 1) * HEAD_DIM) for h in heads]
    lane = lax.broadcasted_iota(jnp.int32, (blk, HEAD_DIM), 1)

    @pl.when(qi == 0)
    def _():
        ones_row = jnp.where(lax.broadcasted_iota(jnp.int32, (16, pair), 0) == 0, 1.0, 0.0).astype(BF16)
        for h in heads:
            for j in range(nb + 2):
                rows = slice(j * blk, (j + 1) * blk)
                if j < nb:
                    kj = k_ref[rows, cols[h]]
                    km_scr[h, j:j + 1, :] = jnp.sum(kj, axis=0, keepdims=True) * (1.0 / blk)
                    ka_scr[h, rows, 0:HEAD_DIM] = kj.astype(BF16)
                else:
                    ka_scr[h, rows, 0:HEAD_DIM] = jnp.zeros((blk, HEAD_DIM), BF16)
                ka_scr[h, rows, HEAD_DIM:] = jnp.where(lane == min(j, nb), 1.0, 0.0).astype(BF16)
            for j in range(nb // 2):
                vt_scr[h, j, 0:HEAD_DIM, :] = v_ref[j * pair:(j + 1) * pair, cols[h]].T.astype(BF16)
                vt_scr[h, j, HEAD_DIM:, :] = ones_row
            qa_scr[h, HEAD_DIM + nbp:, :] = jnp.zeros((HEAD_DIM - nbp, blk), BF16)

    n_idx = lax.broadcasted_iota(jnp.int32, (nb, blk), 0)
    for h in heads:
        q = q_ref[:, cols[h]]
        qa_scr[h, 0:HEAD_DIM, :] = (q * (HEAD_DIM ** -0.5 * LOG2_E)).T.astype(BF16)
        score = _dot3(km_scr[h], (q * (HEAD_DIM ** -0.5)).T)
        cnt = jnp.zeros((nb, blk), F32)
        for m in range(nb):
            row = score[m:m + 1, :]
            beats = jnp.where(row > score, 1.0, jnp.where(row == score, jnp.where(m < n_idx, 1.0, 0.0), 0.0))
            cnt = cnt + jnp.where(m < qi, beats, 0.0)
        past = jnp.where(cnt < MOBA_TOPK, 0.0, MASKED)
        bias = jnp.where(n_idx < qi, past, jnp.where(n_idx == qi, 0.0, MASKED))
        bias = jnp.concatenate([bias, jnp.full((nbp - nb, blk), MASKED, F32)], axis=0)
        qa_scr[h, HEAD_DIM:HEAD_DIM + nbp, :] = bias.astype(BF16)

    def pair_scores(h, t):
        return _dot(ka_scr[h, pl.ds(pl.multiple_of(t * pair, pair), pair), :], qa_scr[h])

    def online_step(state, h, slot, t):
        m_run, acc = state
        s = s_scr[slot, h]
        m_new = jnp.maximum(m_run, jnp.max(s, axis=0, keepdims=True))
        p = jnp.exp2(s - m_new).astype(BF16)
        return [m_new, jnp.exp2(m_run - m_new) * acc + _dot(vt_scr[h, t], p)]

    t_own = lax.shift_right_logical(qi, 1)
    last = jnp.maximum(t_own - 1, 0)
    rel = lax.broadcasted_iota(jnp.int32, (pair, blk), 0) - (qi - 2 * t_own) * blk
    qry_i = lax.broadcasted_iota(jnp.int32, (pair, blk), 1)
    for h in heads:
        s = pair_scores(h, t_own)
        s_scr[0, h] = jnp.where(rel > qry_i, jnp.where(rel < blk, MASKED, s), s)

    def body(u, carry):
        t1 = 2 * u
        k1 = jnp.where(t1 < t_own, t1, nb // 2)
        v0 = jnp.where(u == 0, t_own, t1 - 1)
        v1 = jnp.minimum(t1, last)
        k2 = jnp.minimum(t1 + 1, last)
        state = [list(carry[2 * h:2 * h + 2]) for h in heads]
        for h in heads:
            s_scr[1, h] = pair_scores(h, k1)
        for h in heads:
            state[h] = online_step(state[h], h, 0, v0)
        for h in heads:
            s_scr[0, h] = pair_scores(h, k2)
        for h in heads:
            state[h] = online_step(state[h], h, 1, v1)
        return tuple(x for st in state for x in st)

    init = [jnp.full((1, blk), MASKED, F32), jnp.zeros((HEAD_DIM + 16, blk), F32)] * hp
    fin = lax.fori_loop(0, lax.shift_right_logical(t_own + 2, 1), body, tuple(init))
    for h in heads:
        acc = fin[2 * h + 1]
        o_ref[:, cols[h]] = (acc[0:HEAD_DIM, :] / acc[HEAD_DIM:HEAD_DIM + 1, :]).T


def _moba_prefill(proj3):
    b, t, _ = proj3.shape
    blk = MOBA_BLOCK
    nb = t // blk
    hp = MOBA_HEADS_PER_STEP
    assert t % blk == 0 and nb % 8 == 0 and N_HEADS % hp == 0
    w = hp * HEAD_DIM
    q_col, k_col, v_col = (c * HEAD_DIM // w for c in (16, 24, 32))
    nbp = -(-(nb + 1) // 16) * 16
    assert nbp <= HEAD_DIM
    return pl.pallas_call(
        functools.partial(_moba_kernel, nb=nb, nbp=nbp, hp=hp),
        out_shape=jax.ShapeDtypeStruct((b, t, N_HEADS * HEAD_DIM), F32),
        grid=(b, N_HEADS // hp, nb),
        in_specs=[
            pl.BlockSpec((None, blk, w), lambda bi, h, qi: (bi, qi, q_col + h)),
            pl.BlockSpec((None, t, w), lambda bi, h, qi: (bi, 0, k_col + h)),
            pl.BlockSpec((None, t, w), lambda bi, h, qi: (bi, 0, v_col + h)),
        ],
        out_specs=pl.BlockSpec((None, blk, w), lambda bi, h, qi: (bi, qi, h)),
        scratch_shapes=[
            pltpu.VMEM((hp, t + 2 * blk, 2 * HEAD_DIM), BF16),
            pltpu.VMEM((hp, 2 * HEAD_DIM, blk), BF16),
            pltpu.VMEM((hp, nb // 2, HEAD_DIM + 16, 2 * blk), BF16),
            pltpu.VMEM((hp, nb, HEAD_DIM), F32),
            pltpu.VMEM((2, hp, 2 * blk, blk), F32),
        ],
        compiler_params=_params("arbitrary", "arbitrary", "arbitrary"),
        name="moba",
    )(proj3, proj3, proj3)


def _paged_kmean_kernel(pt_ref, *refs, ppb):
    del pt_ref
    pages, o_ref = refs[:ppb], refs[ppb]
    for b in range(ppb // 2):
        tot = jnp.sum(pages[2 * b][...], axis=0) + jnp.sum(pages[2 * b + 1][...], axis=0)
        o_ref[b] = tot * (1.0 / MOBA_BLOCK)


def _paged_kmean(cache_k, page_table, layer):
    _, _, page, nh, dh = cache_k.shape
    s, n_pages = page_table.shape
    assert 2 * page == MOBA_BLOCK
    ppb = 16 if n_pages % 16 == 0 else 2
    page_specs = [
        pl.BlockSpec((None, None, page, nh, dh),
                     lambda si, c, pt, i=i: (layer, pt[si, c * ppb + i], 0, 0, 0))
        for i in range(ppb)
    ]
    return pl.pallas_call(
        functools.partial(_paged_kmean_kernel, ppb=ppb),
        out_shape=jax.ShapeDtypeStruct((s, n_pages // 2, nh, dh), F32),
        grid_spec=pltpu.PrefetchScalarGridSpec(
            num_scalar_prefetch=1,
            grid=(s, n_pages // ppb),
            in_specs=page_specs,
            out_specs=pl.BlockSpec((None, ppb // 2, nh, dh), lambda si, c, pt: (si, c, 0, 0)),
        ),
        compiler_params=_params("arbitrary", "arbitrary"),
        name="paged_kmean",
    )(page_table, *([cache_k] * ppb))


def _dec_select_kernel(km_ref, q_ref, o_ref, *, n_blocks):
    si = pl.program_id(0)
    q = q_ref[pl.ds(si, 1), :] * (HEAD_DIM ** -0.5)
    lane = lax.broadcasted_iota(jnp.int32, (8, LANES), 1).astype(F32)
    for h in range(N_HEADS):
        qh = jnp.broadcast_to(q[:, h * HEAD_DIM:(h + 1) * HEAD_DIM], (8, HEAD_DIM))
        sc = _dot3(qh, km_ref[h], _NT)
        sc = jnp.where(lane < n_blocks, sc, NEG_INF)
        row = jnp.zeros((8, LANES), F32)
        for r in range(MOBA_TOPK):
            best = jnp.max(sc, axis=1, keepdims=True)
            idx = jnp.min(jnp.where(sc == best, lane, 1e9), axis=1, keepdims=True)
            row = jnp.where(lane == r, idx, row)
            sc = jnp.where(lane == idx, NEG_INF, sc)
        o_ref[pl.ds(h, 1), :] = row[0:1, :].astype(jnp.int32)


def _dec_select(kmean, proj_s):
    s, n_blocks, nh, dh = kmean.shape
    assert MOBA_TOPK <= n_blocks <= LANES
    km = jnp.transpose(kmean, (0, 2, 1, 3))
    km = jnp.pad(km, ((0, 0), (0, 0), (0, LANES - n_blocks), (0, 0)))
    return pl.pallas_call(
        functools.partial(_dec_select_kernel, n_blocks=n_blocks),
        out_shape=jax.ShapeDtypeStruct((s, nh, LANES), jnp.int32),
        grid=(s,),
        in_specs=[
            pl.BlockSpec((None, nh, LANES, dh), lambda si: (si, 0, 0, 0)),
            pl.BlockSpec((s, nh * dh), lambda si: (0, 2)),
        ],
        out_specs=pl.BlockSpec((None, nh, LANES), lambda si: (si, 0, 0)),
        compiler_params=_params("arbitrary"),
        name="dec_select",
    )(km, proj_s)


def _dec_attn_kernel(sel_ref, pt_ref, ck_hbm, cv_hbm, q_ref, kn_ref, vn_ref, o_ref,
                     k_scr, v_scr, k_sem, v_sem, *, layer):
    si = pl.program_id(0)
    n_sel = 2 * MOBA_TOPK

    def slab_copies(h):
        for i in range(n_sel):
            j, r = divmod(i, 2)
            pid = pt_ref[si, 2 * sel_ref[(si * N_HEADS + h) * MOBA_TOPK + j] + r]
            yield pltpu.make_async_copy(ck_hbm.at[layer, pid, :, h, :], k_scr.at[h * n_sel + i], k_sem)
            yield pltpu.make_async_copy(cv_hbm.at[layer, pid, :, h, :], v_scr.at[h * n_sel + i], v_sem)

    for h in range(N_HEADS):
        for cp in slab_copies(h):
            cp.start()
    n_slabs = k_scr.shape[0]
    pltpu.make_async_copy(ck_hbm.at[layer, pl.ds(0, n_slabs), :, 0, :], k_scr, k_sem).wait()
    pltpu.make_async_copy(cv_hbm.at[layer, pl.ds(0, n_slabs), :, 0, :], v_scr, v_sem).wait()

    for h in range(N_HEADS):
        q = q_ref[h:h + 1, :] * (HEAD_DIM ** -0.5)
        s_own = jnp.sum(q * kn_ref[h:h + 1, :], axis=-1, keepdims=True)
        s_sel = [jnp.sum(k_scr[h * n_sel + i] * q, axis=-1, keepdims=True) for i in range(n_sel)]
        m = s_own
        for s_i in s_sel:
            m = jnp.maximum(m, jnp.max(s_i, axis=0, keepdims=True))
        p_own = jnp.exp(s_own - m)
        denom = p_own
        out = p_own * vn_ref[h:h + 1, :]
        for i in range(n_sel):
            p = jnp.exp(s_sel[i] - m)
            denom = denom + jnp.sum(p, axis=0, keepdims=True)
            out = out + jnp.sum(p * v_scr[h * n_sel + i], axis=0, keepdims=True)
        o_ref[h:h + 1, :] = out / denom


def _dec_attn(cache_k, cache_v, page_table, sel, q3, k3, v3, layer):
    _, n_phys, page, nh, dh = cache_k.shape
    s = page_table.shape[0]
    sel_flat = sel[:, :, :MOBA_TOPK].reshape(-1)
    n_slabs = nh * 2 * MOBA_TOPK
    assert nh == N_HEADS and n_phys >= n_slabs
    row_spec = pl.BlockSpec((None, nh, dh), lambda si, sel_r, pt_r: (si, 0, 0))
    return pl.pallas_call(
        functools.partial(_dec_attn_kernel, layer=layer),
        out_shape=jax.ShapeDtypeStruct((s, nh, dh), F32),
        grid_spec=pltpu.PrefetchScalarGridSpec(
            num_scalar_prefetch=2,
            grid=(s,),
            in_specs=[pl.BlockSpec(memory_space=pl.ANY)] * 2 + [row_spec] * 3,
            out_specs=row_spec,
            scratch_shapes=[pltpu.VMEM((n_slabs, page, dh), F32), pltpu.VMEM((n_slabs, page, dh), F32),
                            pltpu.SemaphoreType.DMA, pltpu.SemaphoreType.DMA],
        ),
        compiler_params=_params("arbitrary"),
        name="dec_attn",
    )(sel_flat, page_table, cache_k, cache_v, q3, k3, v3)


def _mix_kernel(u_ref, va_ref, attn_ref, ga0_ref, ga1_ref, gb0_ref, gb1_ref, x_ref, g1_ref,
                ws_ref, bs_ref, lvg_ref, lvb_ref, wba_ref, wbb_ref, wo_ref, l1g_ref, l1b_ref,
                *out_refs, decode, alpha):
    vn = _layer_norm(va_ref[...], lvg_ref[...], lvb_ref[...])
    if decode:
        out_refs[1][...] = vn
        mixed = vn * ws_ref[...] + bs_ref[...]
    else:
        rows = vn.shape[0]
        vnb = vn.astype(BF16)
        r_i = lax.broadcasted_iota(jnp.int32, (CHUNK, CHUNK), 0)
        c_i = lax.broadcasted_iota(jnp.int32, (CHUNK, CHUNK), 1)
        w_tril = [jnp.where(c_i <= r_i, ws_ref[g], 0.0).astype(BF16) for g in range(SGU_GROUPS)]
        gd = vn.shape[1] // SGU_GROUPS
        chunks = []
        for c in range(rows // CHUNK):
            cols = []
            for g in range(SGU_GROUPS):
                v_cg = vnb[c * CHUNK:(c + 1) * CHUNK, g * gd:(g + 1) * gd]
                cols.append(_dot(w_tril[g], v_cg) + bs_ref[:, g:g + 1])
            chunks.append(jnp.concatenate(cols, axis=1))
        mixed = jnp.concatenate(chunks, axis=0)
    a_pre = (u_ref[...] * mixed).astype(BF16)
    a_out = _dot(a_pre, wba_ref[...])
    b_out = _dot(attn_ref[...].astype(BF16), wbb_ref[...])
    g_a = jnp.concatenate([ga0_ref[...], ga1_ref[...]], axis=1)
    g_b = jnp.concatenate([gb0_ref[...], gb1_ref[...]], axis=1)
    merged = (g_a * a_out + g_b * b_out).astype(BF16)
    y = _dot(merged, wo_ref[...])
    out_refs[0][...] = _layer_norm(alpha * x_ref[...] + (1.0 + g1_ref[...]) * y, l1g_ref[...], l1b_ref[...])


def _mix(proj, attn, x, mod3, w_s, b_s, ln_v_g, ln_v_b, w_ba_b, w_bb_b, w_o_b, ln1_g, ln1_b,
         *, decode, seq_len, alpha):
    m, d = x.shape
    sw = w_ba_b.shape[0]
    bm = m if decode else min(256, seq_len)
    tps = 1 if decode else seq_len // bm
    const = lambda *_: (0, 0)
    single = pl.Buffered(1)

    def col(cidx):
        return pl.BlockSpec((bm, sw), lambda i: (i, cidx))

    if decode:
        gd = sw // SGU_GROUPS
        ws_arg = jnp.repeat(w_s[:, 0, 0], gd)[None, :]
        bs_arg = jnp.repeat(b_s[:, 0], gd)[None, :]
        ws_spec = pl.BlockSpec((1, sw), const)
        bs_spec = pl.BlockSpec((1, sw), const)
        out_shape = (jax.ShapeDtypeStruct((m, d), F32), jax.ShapeDtypeStruct((m, sw), F32))
        out_specs = (pl.BlockSpec((bm, d), lambda i: (i, 0)), pl.BlockSpec((bm, sw), lambda i: (i, 0)))
    else:
        ws_arg, bs_arg = w_s, b_s.T
        ws_spec = pl.BlockSpec(w_s.shape, lambda i: (0, 0, 0))
        bs_spec = pl.BlockSpec((CHUNK, SGU_GROUPS), const)
        out_shape = (jax.ShapeDtypeStruct((m, d), F32),)
        out_specs = (pl.BlockSpec((bm, d), lambda i: (i, 0)),)

    return pl.pallas_call(
        functools.partial(_mix_kernel, decode=decode, alpha=alpha),
        out_shape=out_shape,
        grid=(m // bm,),
        in_specs=[
            col(0), col(1),
            pl.BlockSpec((bm, sw), lambda i: (i, 0)),
            col(5), col(6), col(7), col(8),
            pl.BlockSpec((bm, d), lambda i: (i, 0)),
            _mod_spec(mod3, 2, d, decode, tps),
            ws_spec, bs_spec,
            pl.BlockSpec((1, sw), const), pl.BlockSpec((1, sw), const),
            pl.BlockSpec((sw, d), const, pipeline_mode=single),
            pl.BlockSpec((sw, d), const, pipeline_mode=single),
            pl.BlockSpec((d, d), const, pipeline_mode=single),
            pl.BlockSpec((1, d), const), pl.BlockSpec((1, d), const),
        ],
        out_specs=out_specs,
        compiler_params=_params("arbitrary"),
        name="mix_dec" if decode else "mix",
    )(proj, proj, attn, proj, proj, proj, proj, x, mod3, ws_arg, bs_arg,
      ln_v_g[None, :], ln_v_b[None, :], w_ba_b, w_bb_b, w_o_b, ln1_g[None, :], ln1_b[None, :])


def _ffn_kernel(x_ref, sc_ref, sh_ref, g_ref, wg_ref, wu_ref, wd_ref, lg_ref, lb_ref, o_ref,
                h_scr, acc_scr, *, alpha):
    f = pl.program_id(1)

    @pl.when(f == 0)
    def _():
        h_scr[...] = (x_ref[...] * (1.0 + sc_ref[...]) + sh_ref[...]).astype(BF16)
        acc_scr[...] = jnp.zeros_like(acc_scr)

    h = h_scr[...]
    a = _dot(h, wg_ref[...])
    act = (a * _sigmoid(a) * _dot(h, wu_ref[...])).astype(BF16)
    acc_scr[...] += _dot(act, wd_ref[...])

    @pl.when(f == pl.num_programs(1) - 1)
    def _():
        y = alpha * x_ref[...] + (1.0 + g_ref[...]) * acc_scr[...]
        o_ref[...] = _layer_norm(y, lg_ref[...], lb_ref[...])


def _ffn(x, mod3, wg_b, wu_b, wd_b, ln_g, ln_b, *, decode, seq_len, alpha):
    m, d = x.shape
    ff = wg_b.shape[1]
    bf = 512 if ff % 512 == 0 else ff
    bm = m if decode else min(512, seq_len)
    tps = 1 if decode else seq_len // bm
    const = lambda *_: (0, 0)
    return pl.pallas_call(
        functools.partial(_ffn_kernel, alpha=alpha),
        out_shape=jax.ShapeDtypeStruct((m, d), F32),
        grid=(m // bm, ff // bf),
        in_specs=[
            pl.BlockSpec((bm, d), lambda i, f: (i, 0)),
            _mod_spec(mod3, 4, d, decode, tps),
            _mod_spec(mod3, 3, d, decode, tps),
            _mod_spec(mod3, 5, d, decode, tps),
            pl.BlockSpec((d, bf), lambda i, f: (0, f)),
            pl.BlockSpec((d, bf), lambda i, f: (0, f)),
            pl.BlockSpec((bf, d), lambda i, f: (f, 0)),
            pl.BlockSpec((1, d), const), pl.BlockSpec((1, d), const),
        ],
        out_specs=pl.BlockSpec((bm, d), lambda i, f: (i, 0)),
        scratch_shapes=[pltpu.VMEM((bm, d), BF16), pltpu.VMEM((bm, d), F32)],
        compiler_params=_params("arbitrary", "arbitrary"),
        name="ffn_dec" if decode else "ffn",
    )(x, mod3, mod3, mod3, wg_b, wu_b, wd_b, ln_g[None, :], ln_b[None, :])


def _moe_route_kernel(x_ref, sc_ref, sh_ref, r_ref, cnt0_ref, h_ref, route_ref, cnt_ref, cnt_scr):
    i = pl.program_id(0)
    rows = x_ref.shape[0]

    @pl.when(i == 0)
    def _():
        cnt_scr[...] = cnt0_ref[...]

    h = x_ref[...] * (1.0 + sc_ref[...]) + sh_ref[...]
    h_ref[...] = h
    lane = lax.broadcasted_iota(jnp.int32, (rows, LANES), 1)
    lane_f = lane.astype(F32)
    logits = jnp.where(lane < N_EXPERTS, _dot3(h, r_ref[...]), NEG_INF)
    v1 = jnp.max(logits, axis=1, keepdims=True)
    i1 = jnp.min(jnp.where(logits == v1, lane_f, 1e9), axis=1, keepdims=True)
    rest = jnp.where(lane_f == i1, NEG_INF, logits)
    v2 = jnp.max(rest, axis=1, keepdims=True)
    i2 = jnp.min(jnp.where(rest == v2, lane_f, 1e9), axis=1, keepdims=True)
    ex = jnp.exp(v2 - v1)
    w1 = 1.0 / (1.0 + ex)
    w2 = ex / (1.0 + ex)
    picks = jnp.where(lane_f == i1, 1.0, jnp.where(lane_f == i2, 1.0, 0.0))
    if rows >= LANES:
        r_i = lax.broadcasted_iota(jnp.int32, (rows, rows), 0)
        c_i = lax.broadcasted_iota(jnp.int32, (rows, rows), 1)
        before = _dot(jnp.where(c_i < r_i, 1.0, 0.0).astype(BF16), picks.astype(BF16))
    else:
        r_i = lax.broadcasted_iota(jnp.int32, (rows, LANES), 0)
        before = jnp.zeros((rows, LANES), F32)
        for t in range(rows - 1):
            before = before + jnp.where(r_i > t, picks[t:t + 1, :], 0.0)
    before = before + cnt_scr[0:1, :]
    rank1 = jnp.sum(jnp.where(lane_f == i1, before, 0.0), axis=1, keepdims=True)
    rank2 = jnp.sum(jnp.where(lane_f == i2, before, 0.0), axis=1, keepdims=True)
    rec = jnp.zeros((rows, LANES), F32)
    for k, val in enumerate((i1, i2, w1, w2, rank1, rank2)):
        rec = jnp.where(lane == k, val, rec)
    route_ref[...] = rec
    cnt_scr[...] = cnt_scr[...] + jnp.sum(picks, axis=0, keepdims=True)
    cnt_ref[...] = cnt_scr[...]


def _moe_route(x, mod3, router, cnt0, *, decode, seq_len):
    m, d = x.shape
    bm = m if decode else min(512, seq_len)
    tps = 1 if decode else seq_len // bm
    router_p = jnp.pad(router, ((0, 0), (0, LANES - router.shape[1])))
    return pl.pallas_call(
        _moe_route_kernel,
        out_shape=(jax.ShapeDtypeStruct((m, d), F32), jax.ShapeDtypeStruct((m, LANES), F32),
                   jax.ShapeDtypeStruct((8, LANES), F32)),
        grid=(m // bm,),
        in_specs=[
            pl.BlockSpec((bm, d), lambda i: (i, 0)),
            _mod_spec(mod3, 4, d, decode, tps),
            _mod_spec(mod3, 3, d, decode, tps),
            pl.BlockSpec((d, LANES), lambda i: (0, 0)),
            pl.BlockSpec((8, LANES), lambda i: (0, 0)),
        ],
        out_specs=(pl.BlockSpec((bm, d), lambda i: (i, 0)), pl.BlockSpec((bm, LANES), lambda i: (i, 0)),
                   pl.BlockSpec((8, LANES), lambda i: (0, 0))),
        scratch_shapes=[pltpu.VMEM((8, LANES), F32)],
        compiler_params=_params("arbitrary"),
        name="moe_route_dec" if decode else "moe_route",
    )(x, mod3, mod3, router_p, cnt0)


def _row_gather_start(src_hbm, idx_ref, base, dst, sem):
    def issue(r, carry):
        pltpu.make_async_copy(src_hbm.at[pl.ds(idx_ref[base + r], 1), :], dst.at[pl.ds(r, 1), :], sem).start()
        return carry

    lax.fori_loop(0, dst.shape[0], issue, 0, unroll=8)


def _row_gather_wait(src_hbm, dst, sem):
    pltpu.make_async_copy(src_hbm.at[pl.ds(0, dst.shape[0]), :], dst, sem).wait()


MOE_HALF_ROWS = 512


def _moe_group_kernel(te_ref, nh_ref, tok_ref, h_hbm, wg_ref, wu_ref, wd_ref, y_ref, xg_scr, hb_scr, sem):
    del te_ref
    i = pl.program_id(0)
    f = pl.program_id(1)
    half = xg_scr.shape[0]
    n_halves = nh_ref[i]

    @pl.when(f == 0)
    def _():
        y_ref[...] = jnp.zeros_like(y_ref)

    @pl.when((f == 0) & (n_halves > 0) & (i == 0))
    def _():
        _row_gather_start(h_hbm, tok_ref, 0, xg_scr, sem)

    @pl.when((f == 0) & (n_halves > 0))
    def _():
        _row_gather_wait(h_hbm, xg_scr, sem)
        hb_scr[0] = xg_scr[...].astype(BF16)

    @pl.when((f == 0) & (n_halves > 1))
    def _():
        _row_gather_start(h_hbm, tok_ref, (2 * i + 1) * half, xg_scr, sem)
        _row_gather_wait(h_hbm, xg_scr, sem)
        hb_scr[1] = xg_scr[...].astype(BF16)

    def compute(halves):
        wg, wu, wd = (w[...].astype(BF16) for w in (wg_ref, wu_ref, wd_ref))
        for hf in range(halves):
            h = hb_scr[hf]
            a = _dot(h, wg)
            act = (a * _sigmoid(a) * _dot(h, wu)).astype(BF16)
            y_ref[hf * half:(hf + 1) * half, :] += _dot(act, wd)

    pl.when(n_halves == 1)(lambda: compute(1))
    pl.when(n_halves == 2)(lambda: compute(2))

    n_tiles, n_f = pl.num_programs(0), pl.num_programs(1)
    nxt = jnp.minimum(i + 1, n_tiles - 1)
    ahead = jnp.minimum(1, n_f - 1)

    @pl.when((f == ahead) & (i + 1 < n_tiles) & (nh_ref[nxt] > 0))
    def _():
        _row_gather_start(h_hbm, tok_ref, 2 * nxt * half, xg_scr, sem)


def _moe_group(h_all, tile_expert, tile_halves, tok_of_slot, wg, wu, wd):
    d = h_all.shape[1]
    ff = wg.shape[2]
    bf = 512 if ff % 512 == 0 else ff
    nf = ff // bf
    half = MOE_HALF_ROWS
    n_tiles = tile_expert.shape[0]

    def f_idx(i, f, nh):
        return jnp.where(nh[i] > 0, f, nf - 1)

    return pl.pallas_call(
        _moe_group_kernel,
        out_shape=jax.ShapeDtypeStruct((n_tiles * 2 * half, d), F32),
        grid_spec=pltpu.PrefetchScalarGridSpec(
            num_scalar_prefetch=3,
            grid=(n_tiles, nf),
            in_specs=[
                pl.BlockSpec(memory_space=pl.ANY),
                pl.BlockSpec((None, d, bf), lambda i, f, te, nh, tok: (te[i], 0, f_idx(i, f, nh))),
                pl.BlockSpec((None, d, bf), lambda i, f, te, nh, tok: (te[i], 0, f_idx(i, f, nh))),
                pl.BlockSpec((None, bf, d), lambda i, f, te, nh, tok: (te[i], f_idx(i, f, nh), 0)),
            ],
            out_specs=pl.BlockSpec((2 * half, d), lambda i, f, te, nh, tok: (i, 0)),
            scratch_shapes=[pltpu.VMEM((half, d), F32), pltpu.VMEM((2, half, d), BF16),
                            pltpu.SemaphoreType.DMA],
        ),
        compiler_params=_params("arbitrary", "arbitrary"),
        name="moe_group",
    )(tile_expert, tile_halves, tok_of_slot, h_all, wg, wu, wd)


def _moe_combine_kernel(s1_ref, s2_ref, y_hbm, route_ref, x_ref, g_ref, lg_ref, lb_ref, o_ref,
                        y1_scr, y2_scr, sem1, sem2, *, alpha):
    i = pl.program_id(0)
    rows = x_ref.shape[0]
    _row_gather_start(y_hbm, s1_ref, i * rows, y1_scr, sem1)
    _row_gather_start(y_hbm, s2_ref, i * rows, y2_scr, sem2)
    _row_gather_wait(y_hbm, y1_scr, sem1)
    _row_gather_wait(y_hbm, y2_scr, sem2)
    route = route_ref[...]
    f = route[:, 2:3] * y1_scr[...] + route[:, 3:4] * y2_scr[...]
    y = alpha * x_ref[...] + (1.0 + g_ref[...]) * f
    o_ref[...] = _layer_norm(y, lg_ref[...], lb_ref[...])


def _moe_combine(y_sorted, slot1, slot2, route, x, mod3, ln_g, ln_b, *, decode, seq_len, alpha):
    m, d = x.shape
    bm = m if decode else min(256, seq_len)
    tps = 1 if decode else seq_len // bm
    rows_mod = mod3.shape[1]
    if decode:
        g_spec = pl.BlockSpec((None, rows_mod, d), lambda i, s1, s2: (0, 0, 5))
    else:
        g_spec = pl.BlockSpec((None, rows_mod, d), lambda i, s1, s2: (i // tps, 0, 5))
    const = lambda i, s1, s2: (0, 0)
    return pl.pallas_call(
        functools.partial(_moe_combine_kernel, alpha=alpha),
        out_shape=jax.ShapeDtypeStruct((m, d), F32),
        grid_spec=pltpu.PrefetchScalarGridSpec(
            num_scalar_prefetch=2,
            grid=(m // bm,),
            in_specs=[
                pl.BlockSpec(memory_space=pl.ANY),
                pl.BlockSpec((bm, LANES), lambda i, s1, s2: (i, 0)),
                pl.BlockSpec((bm, d), lambda i, s1, s2: (i, 0)),
                g_spec,
                pl.BlockSpec((1, d), const), pl.BlockSpec((1, d), const),
            ],
            out_specs=pl.BlockSpec((bm, d), lambda i, s1, s2: (i, 0)),
            scratch_shapes=[pltpu.VMEM((bm, d), F32), pltpu.VMEM((bm, d), F32),
                            pltpu.SemaphoreType.DMA, pltpu.SemaphoreType.DMA],
        ),
        compiler_params=_params("arbitrary"),
        name="moe_combine_dec" if decode else "moe_combine",
    )(slot1, slot2, y_sorted, route, x, mod3, ln_g[None, :], ln_b[None, :])


def _moe(xp, xs, mod_p, mod_s, router, wg, wu, wd, ln_g, ln_b, *, seq_len, alpha):
    mp, ms = xp.shape[0], xs.shape[0]
    m = mp + ms
    ne = wg.shape[0]
    half = MOE_HALF_ROWS
    n_tiles = -(-m // half) + ne
    h_p, route_p, cnt_p = _moe_route(xp, mod_p, router, jnp.zeros((8, LANES), F32), decode=False, seq_len=seq_len)
    h_s, route_s, cnt = _moe_route(xs, mod_s, router, cnt_p, decode=True, seq_len=1)
    h_all = jnp.concatenate([h_p, h_s], axis=0)
    route = jnp.concatenate([route_p, route_s], axis=0)

    counts = cnt[0, :ne].astype(jnp.int32)
    n_half = (counts + half - 1) // half
    cap = (n_half + 1) // 2 * (2 * half)
    ends = jnp.cumsum(cap)
    starts = ends - cap
    used_end = starts + n_half * half
    e1, e2 = route[:, 0].astype(jnp.int32), route[:, 1].astype(jnp.int32)
    slot1 = starts[e1] + route[:, 4].astype(jnp.int32)
    slot2 = starts[e2] + route[:, 5].astype(jnp.int32)
    tok = jnp.arange(m, dtype=jnp.int32)
    tok_of_slot = jnp.zeros((n_tiles * 2 * half,), jnp.int32).at[jnp.concatenate([slot1, slot2])].set(
        jnp.concatenate([tok, tok]))
    tile_lo = jnp.arange(n_tiles, dtype=jnp.int32) * (2 * half)
    last_lo = jnp.maximum(ends[-1] - 2 * half, 0)
    tile_expert = jnp.sum((jnp.minimum(tile_lo, last_lo)[:, None] >= ends[None, :]).astype(jnp.int32), axis=1)
    tile_expert = jnp.minimum(tile_expert, ne - 1)
    tile_halves = jnp.where(tile_lo >= ends[-1], 0, jnp.where(tile_lo + half < used_end[tile_expert], 2, 1))

    y_sorted = _moe_group(h_all, tile_expert, tile_halves.astype(jnp.int32), tok_of_slot, wg, wu, wd)
    out_p = _moe_combine(y_sorted, slot1[:mp], slot2[:mp], route_p, xp, mod_p, ln_g, ln_b,
                         decode=False, seq_len=seq_len, alpha=alpha)
    out_s = _moe_combine(y_sorted, slot1[mp:], slot2[mp:], route_s, xs, mod_s, ln_g, ln_b,
                         decode=True, seq_len=1, alpha=alpha)
    return out_p, out_s


def kernel(x_prompt, x_sample, cache_k, cache_v, page_table, c_prompt, c_sample,
           w_in, ln_v_g, ln_v_b, w_s, b_s, w_ba, w_bb, w_o, w_ada, b_ada,
           ln1_g, ln1_b, ln2_g, ln2_b, ffn_w_gate, ffn_w_up, ffn_w_down,
           moe_router, moe_w_gate, moe_w_up, moe_w_down):
    depth = w_in.shape[0]
    b, t, d = x_prompt.shape
    s = x_sample.shape[0]
    assert x_sample.shape[1] == 1 and s % 8 == 0 and b + s <= ADA_ROWS
    n_pages, page = page_table.shape[1], cache_k.shape[2]
    past_len = n_pages * page
    alpha = (2 * depth) ** 0.25
    sw = w_ba.shape[1]

    c_all = jnp.concatenate([c_prompt, c_sample, jnp.zeros((ADA_ROWS - b - s, d), F32)], axis=0)
    mod_all = _ada(c_all, w_ada, b_ada)

    tabs_p = _rope_tables(jnp.arange(t))
    tabs_s = _rope_tables(jnp.full((s,), past_len, jnp.int32))

    xp = x_prompt.reshape(b * t, d)
    xs = x_sample.reshape(s, d)
    k_p, v_p, k_s, v_s, vn_s = [], [], [], [], []
    aw = N_HEADS * HEAD_DIM
    q_lo = 2 * sw
    for l in range(depth):
        mod_p = mod_all[l, :b][:, None, :]
        mod_s = mod_all[l, b:b + s][None]
        w_in_b = w_in[l].astype(BF16)
        mix_w = (w_s[l], b_s[l], ln_v_g[l], ln_v_b[l], w_ba[l].astype(BF16), w_bb[l].astype(BF16),
                 w_o[l].astype(BF16), ln1_g[l], ln1_b[l])
        proj_p, k_new, v_new = _proj(xp, mod_p, w_in_b, tabs_p, decode=False, seq_len=t)
        attn_p = _moba_prefill(proj_p.reshape(b, t, -1)).reshape(b * t, -1)
        (xp1,) = _mix(proj_p, attn_p, xp, mod_p, *mix_w, decode=False, seq_len=t, alpha=alpha)
        k_p.append(k_new.reshape(b, t, N_HEADS, HEAD_DIM))
        v_p.append(v_new.reshape(b, t, N_HEADS, HEAD_DIM))

        proj_s, k3, v3 = _proj(xs, mod_s, w_in_b, tabs_s, decode=True, seq_len=1)
        kmean = _paged_kmean(cache_k, page_table, l)
        sel = _dec_select(kmean, proj_s)
        q3 = proj_s[:, q_lo:q_lo + aw].reshape(s, N_HEADS, HEAD_DIM)
        attn_s = _dec_attn(cache_k, cache_v, page_table, sel, q3, k3, v3, l).reshape(s, aw)
        xs1, vn = _mix(proj_s, attn_s, xs, mod_s, *mix_w, decode=True, seq_len=1, alpha=alpha)

        if l % 2 == 0:
            ffn_w = (ffn_w_gate[l // 2].astype(BF16), ffn_w_up[l // 2].astype(BF16),
                     ffn_w_down[l // 2].astype(BF16), ln2_g[l], ln2_b[l])
            xp = _ffn(xp1, mod_p, *ffn_w, decode=False, seq_len=t, alpha=alpha)
            xs = _ffn(xs1, mod_s, *ffn_w, decode=True, seq_len=1, alpha=alpha)
        else:
            xp, xs = _moe(xp1, xs1, mod_p, mod_s, moe_router[l // 2], moe_w_gate[l // 2], moe_w_up[l // 2],
                          moe_w_down[l // 2], ln2_g[l], ln2_b[l], seq_len=t, alpha=alpha)
        k_s.append(k3.reshape(s, 1, N_HEADS, HEAD_DIM))
        v_s.append(v3.reshape(s, 1, N_HEADS, HEAD_DIM))
        vn_s.append(vn.reshape(s, 1, sw))

    return (xp.reshape(b, t, d), xs.reshape(s, 1, d), jnp.stack(k_p), jnp.stack(v_p),
            jnp.stack(k_s), jnp.stack(v_s), jnp.stack(vn_s))
```
